```python
import math
import jax, jax.numpy as jnp
from jax import lax
import numpy as np

D_MODEL = 1024
BATCH = 4
SEQ = 8192
DEPTH = 2

D_MIX = 2 * D_MODEL
GROUP_W = D_MIX // 4
CHUNK = 128
Q_BLOCK = 128
ROPE_THETA = 10000.0
EPS = 1e-6
NEG_INF = -1e30

RET_HEADS = 4
RET_DK = 64
RET_DV = GROUP_W // RET_HEADS
DIFF_HEADS = 4
DIFF_DK = 64
DIFF_DV = GROUP_W // DIFF_HEADS
SSD_HEAD_DIM = 64
SSD_HEADS = GROUP_W // SSD_HEAD_DIM
SSD_GROUPS = 2
SSD_STATE = 128
SSD_CONV = 4
SSD_XBC = GROUP_W + 2 * SSD_GROUPS * SSD_STATE
MLSTM_HEADS = 4
MLSTM_DH = GROUP_W // MLSTM_HEADS
MLSTM_CONV = 4

RET_COLS = 2 * RET_HEADS * RET_DK + 2 * GROUP_W
DIFF_COLS = 4 * GROUP_W
SSD_COLS = SSD_XBC + SSD_HEADS + GROUP_W
MLSTM_COLS = 2 * GROUP_W + GROUP_W + GROUP_W + 2 * MLSTM_HEADS + GROUP_W
IN_COLS = RET_COLS + DIFF_COLS + SSD_COLS + MLSTM_COLS

kernel_name = 'hymba_style_retention_diffattn_ssd_mlstm'


def split_cols(a, sizes):
    offs = [int(o) for o in np.cumsum(sizes)[:-1]]
    return jnp.split(a, offs, axis=-1)


def _rms(x):
    return x * lax.rsqrt(jnp.mean(jnp.square(x), axis=-1, keepdims=True) + EPS)


def _layernorm(x):
    x = x - jnp.mean(x, axis=-1, keepdims=True)
    return _rms(x)


def rmsnorm(x, w):
    return _rms(x.astype(jnp.float32)) * w.astype(jnp.float32)


def rope_tables(seq, dim):
    inv = ROPE_THETA ** (-jnp.arange(0, dim, 2, dtype=jnp.float32) / dim)
    ang = jnp.arange(seq, dtype=jnp.float32)[:, None] * inv[None, :]
    return jnp.cos(ang), jnp.sin(ang)


def apply_rope(x, cos, sin):
    half = x.shape[-1] // 2
    shp = (cos.shape[0],) + (1,) * (x.ndim - 3) + (half,)
    c, s = cos.reshape(shp), sin.reshape(shp)
    x1, x2 = x[..., :half], x[..., half:]
    return jnp.concatenate([x1 * c - x2 * s, x2 * c + x1 * s], axis=-1)


def causal_conv(x, w, b):
    k, c = w.shape
    out = lax.conv_general_dilated(x, w.astype(x.dtype)[:, None, :], window_strides=(1,),
                                   padding=[(k - 1, 0)], dimension_numbers=('NWC', 'WIO', 'NWC'),
                                   feature_group_count=c)
    return out + b.astype(x.dtype)


def retention_mixer(q, k, v, cos, sin):
    bsz, seq, nh, _ = q.shape
    nc = seq // CHUNK
    q = apply_rope(q, cos, sin)
    k = apply_rope(k, cos, sin) * (RET_DK ** -0.5)
    log_g = jnp.log(1.0 - jnp.exp2(-5.0 - jnp.arange(nh, dtype=jnp.float32)))
    qc = q.reshape(bsz, nc, CHUNK, nh, RET_DK)
    kc = k.reshape(bsz, nc, CHUNK, nh, RET_DK)
    vc = v.reshape(bsz, nc, CHUNK, nh, RET_DV)
    pos = jnp.arange(CHUNK, dtype=jnp.float32)
    rel = pos[:, None] - pos[None, :]
    decay = jnp.where(rel >= 0, jnp.exp(log_g[:, None, None] * jnp.maximum(rel, 0.0)), 0.0)
    scores = jnp.einsum('bclhd,bcshd->bchls', qc, kc) * decay
    intra = jnp.einsum('bchls,bcshe->bclhe', scores, vc)
    k_w = jnp.exp(log_g[None, :] * (CHUNK - 1.0 - pos)[:, None])
    chunk_kv = jnp.einsum('bcshd,sh,bcshe->cbhde', kc, k_w, vc)
    chunk_decay = jnp.exp(log_g * CHUNK)[None, :, None, None]

    def step(state, kv):
        return state * chunk_decay + kv, state

    _, prev = lax.scan(step, jnp.zeros((bsz, nh, RET_DK, RET_DV), jnp.float32), chunk_kv)
    q_w = jnp.exp(log_g[None, :] * (pos + 1.0)[:, None])
    inter = jnp.einsum('bclhd,lh,cbhde->bclhe', qc, q_w, prev)
    out = _rms(intra + inter)
    return out.reshape(bsz, seq, nh * RET_DV)


def diff_attention_mixer(q, k, v, lam, lam_init, norm_w, cos, sin):
    bsz, seq, nh = q.shape[:3]
    nb = seq // Q_BLOCK
    q = apply_rope(q, cos, sin) * (DIFF_DK ** -0.5)
    k = apply_rope(k, cos, sin)
    qb = jnp.moveaxis(q.reshape(bsz, nb, Q_BLOCK, nh, 2, DIFF_DK), 1, 0)
    starts = jnp.arange(nb, dtype=jnp.int32) * Q_BLOCK
    kpos = jnp.arange(seq, dtype=jnp.int32)

    def block(args):
        qi, start = args
        s = jnp.einsum('bqhtd,bkhtd->bhtqk', qi, k)
        mask = kpos[None, :] <= (start + jnp.arange(Q_BLOCK, dtype=jnp.int32))[:, None]
        p = jax.nn.softmax(jnp.where(mask, s, NEG_INF), axis=-1)
        a = p[:, :, 0] - lam * p[:, :, 1]
        return jnp.einsum('bhqk,bkhe->bqhe', a, v)

    out = lax.map(block, (qb, starts))
    out = jnp.moveaxis(out, 0, 1).reshape(bsz, seq, nh, DIFF_DV)
    out = _rms(out) * norm_w * (1.0 - lam_init)
    return out.reshape(bsz, seq, nh * DIFF_DV)


def ssd_mixer(xbc, dt_raw, conv_w, conv_b, dt_bias, a_log, d_skip):
    bsz, seq, _ = xbc.shape
    nc = seq // CHUNK
    hg = SSD_HEADS // SSD_GROUPS
    xbc = jax.nn.silu(causal_conv(xbc, conv_w, conv_b))
    xs, bm, cm = split_cols(xbc, (GROUP_W, SSD_GROUPS * SSD_STATE, SSD_GROUPS * SSD_STATE))
    xs = xs.reshape(bsz, nc, CHUNK, SSD_GROUPS, hg, SSD_HEAD_DIM)
    bm = bm.reshape(bsz, nc, CHUNK, SSD_GROUPS, SSD_STATE)
    cm = cm.reshape(bsz, nc, CHUNK, SSD_GROUPS, SSD_STATE)
    dt = jax.nn.softplus(dt_raw + dt_bias).reshape(bsz, nc, CHUNK, SSD_GROUPS, hg)
    a = -jnp.exp(a_log).reshape(SSD_GROUPS, hg)
    a_cs = jnp.cumsum(jnp.moveaxis(dt * a, 2, -1), axis=-1)
    xdt = xs * dt[..., None]
    causal = jnp.tril(jnp.ones((CHUNK, CHUNK), bool))
    seg = a_cs[..., :, None] - a_cs[..., None, :]
    lmat = jnp.exp(jnp.where(causal, seg, -jnp.inf))
    cb = jnp.einsum('bclgn,bcsgn->bcgls', cm, bm)
    y_diag = jnp.einsum('bcgls,bcgrls,bcsgrp->bclgrp', cb, lmat, xdt)
    decay_states = jnp.exp(a_cs[..., -1:] - a_cs)
    states = jnp.einsum('bcsgn,bcgrs,bcsgrp->cbgrpn', bm, decay_states, xdt)
    chunk_decay = jnp.moveaxis(jnp.exp(a_cs[..., -1]), 1, 0)

    def step(h, inp):
        st, dec = inp
        return h * dec[..., None, None] + st, h

    _, prev = lax.scan(step, jnp.zeros(states.shape[1:], jnp.float32), (states, chunk_decay))
    y_off = jnp.einsum('bclgn,cbgrpn,bcgrl->bclgrp', cm, prev, jnp.exp(a_cs))
    y = y_diag + y_off + xs * d_skip.reshape(SSD_GROUPS, hg)[:, :, None]
    return y.reshape(bsz, seq, GROUP_W)


def mlstm_mixer(qk, v, o_raw, gate_raw, conv_w, conv_b, gate_b, norm_w):
    bsz, seq, _ = v.shape
    nh, dh = MLSTM_HEADS, MLSTM_DH
    nc = seq // CHUNK
    qk = jax.nn.silu(causal_conv(qk, conv_w, conv_b))
    q, k = jnp.split(qk, 2, axis=-1)

    def to_chunks(t):
        return t.reshape(bsz, nc, CHUNK, nh, dh).transpose(1, 0, 3, 2, 4)

    def gate_chunks(t):
        return t.reshape(bsz, nc, CHUNK, nh).transpose(1, 0, 3, 2)

    qc, kc, vc = to_chunks(q), to_chunks(k * (dh ** -0.5)), to_chunks(v)
    g = gate_raw + gate_b
    ic = gate_chunks(g[..., :nh])
    fc = gate_chunks(jax.nn.log_sigmoid(g[..., nh:]))
    causal = jnp.tril(jnp.ones((CHUNK, CHUNK), bool))

    def step(carry, inp):
        c_mat, n_vec, m = carry
        qj, kj, vj, ij, fj = inp
        b = jnp.cumsum(fj, axis=-1)
        d_log = jnp.where(causal, b[..., :, None] - b[..., None, :] + ij[..., None, :], -jnp.inf)
        inter_log = b + m[..., None]
        m_row = jnp.maximum(jnp.max(d_log, axis=-1), inter_log)
        w = jnp.exp(d_log - m_row[..., None])
        s = jnp.einsum('bhld,bhsd->bhls', qj, kj) * w
        inter_w = jnp.exp(inter_log - m_row)
        num = jnp.einsum('bhls,bhse->bhle', s, vj) + inter_w[..., None] * jnp.einsum('bhld,bhed->bhle', qj, c_mat)
        qn = jnp.sum(s, axis=-1) + inter_w * jnp.einsum('bhld,bhd->bhl', qj, n_vec)
        h = num / jnp.maximum(jnp.abs(qn), jnp.exp(-m_row))[..., None]
        b_last = b[..., -1]
        w_log = b_last[..., None] - b + ij
        m_new = jnp.maximum(b_last + m, jnp.max(w_log, axis=-1))
        ws = jnp.exp(w_log - m_new[..., None])
        carry_decay = jnp.exp(b_last + m - m_new)
        c_new = carry_decay[..., None, None] * c_mat + jnp.einsum('bhs,bhse,bhsd->bhed', ws, vj, kj)
        n_new = carry_decay[..., None] * n_vec + jnp.einsum('bhs,bhsd->bhd', ws, kj)
        return (c_new, n_new, m_new), h

    init = (jnp.zeros((bsz, nh, dh, dh), jnp.float32), jnp.zeros((bsz, nh, dh), jnp.float32),
            jnp.zeros((bsz, nh), jnp.float32))
    _, h = lax.scan(step, init, (qc, kc, vc, ic, fc))
    h = h.transpose(1, 0, 3, 2, 4).reshape(bsz, seq, nh, dh)
    h = jax.nn.sigmoid(o_raw).reshape(bsz, seq, nh, dh) * h
    h = _layernorm(h) * norm_w.reshape(nh, dh)
    return h.reshape(bsz, seq, GROUP_W)


def setup_inputs(seed: int = 0) -> dict:
    key = jax.random.key(seed)
    ks = jax.random.split(key, 20)
    f32 = jnp.float32
    nrm = jax.random.normal
    x = nrm(ks[0], (BATCH, SEQ, D_MODEL), f32)
    norm_w = 1.0 + 0.02 * nrm(ks[1], (DEPTH, D_MODEL), f32)
    w_in = nrm(ks[2], (DEPTH, D_MODEL, IN_COLS), f32) * (D_MODEL ** -0.5)
    w_out = nrm(ks[3], (DEPTH, D_MIX, D_MODEL), f32) * (D_MIX ** -0.5)
    diff_lambda = 0.1 * nrm(ks[4], (DEPTH, 4, DIFF_DK), f32)
    diff_norm_w = 1.0 + 0.02 * nrm(ks[5], (DEPTH, DIFF_DV), f32)
    ssd_conv_w = nrm(ks[6], (DEPTH, SSD_CONV, SSD_XBC), f32) * (SSD_CONV ** -0.5)
    ssd_conv_b = 0.02 * nrm(ks[7], (DEPTH, SSD_XBC), f32)
    u = jax.random.uniform(ks[8], (DEPTH, SSD_HEADS), f32)
    dt0 = jnp.exp(u * (math.log(0.1) - math.log(0.001)) + math.log(0.001))
    ssd_dt_bias = dt0 + jnp.log(-jnp.expm1(-dt0))
    ssd_a_log = jnp.log(jax.random.uniform(ks[9], (DEPTH, SSD_HEADS), f32, minval=1.0, maxval=16.0))
    ssd_d = 1.0 + 0.1 * nrm(ks[10], (DEPTH, SSD_HEADS), f32)
    ssd_norm_w = 1.0 + 0.02 * nrm(ks[11], (DEPTH, GROUP_W), f32)
    mlstm_conv_w = nrm(ks[12], (DEPTH, MLSTM_CONV, 2 * GROUP_W), f32) * (MLSTM_CONV ** -0.5)
    mlstm_conv_b = 0.02 * nrm(ks[13], (DEPTH, 2 * GROUP_W), f32)
    i_b = 0.1 * nrm(ks[14], (DEPTH, MLSTM_HEADS), f32)
    f_b = jnp.linspace(3.0, 6.0, MLSTM_HEADS, dtype=f32)[None, :] + 0.1 * nrm(ks[15], (DEPTH, MLSTM_HEADS), f32)
    mlstm_gate_b = jnp.concatenate([i_b, f_b], axis=-1)
    mlstm_norm_w = 1.0 + 0.02 * nrm(ks[16], (DEPTH, GROUP_W), f32)
    final_norm_w = 1.0 + 0.02 * nrm(ks[17], (D_MODEL,), f32)
    return {'x': x, 'norm_w': norm_w, 'w_in': w_in, 'w_out': w_out,
            'diff_lambda': diff_lambda, 'diff_norm_w': diff_norm_w,
            'ssd_conv_w': ssd_conv_w, 'ssd_conv_b': ssd_conv_b, 'ssd_dt_bias': ssd_dt_bias,
            'ssd_a_log': ssd_a_log, 'ssd_d': ssd_d, 'ssd_norm_w': ssd_norm_w,
            'mlstm_conv_w': mlstm_conv_w, 'mlstm_conv_b': mlstm_conv_b,
            'mlstm_gate_b': mlstm_gate_b, 'mlstm_norm_w': mlstm_norm_w,
            'final_norm_w': final_norm_w}


def reference(x, norm_w, w_in, w_out, diff_lambda, diff_norm_w, ssd_conv_w, ssd_conv_b, ssd_dt_bias,
              ssd_a_log, ssd_d, ssd_norm_w, mlstm_conv_w, mlstm_conv_b, mlstm_gate_b, mlstm_norm_w,
              final_norm_w):
    f32 = jnp.float32
    bsz, seq, _ = x.shape
    cos, sin = rope_tables(seq, RET_DK)
    silu = jax.nn.silu
    h = x
    for l in range(DEPTH):
        u = rmsnorm(h, norm_w[l]).astype(x.dtype)
        proj = jnp.einsum('bsd,de->bse', u, w_in[l]).astype(f32)
        ret_p, diff_p, ssd_p, ml_p = split_cols(proj, (RET_COLS, DIFF_COLS, SSD_COLS, MLSTM_COLS))

        rq, rk, rv, rz = split_cols(ret_p, (RET_HEADS * RET_DK, RET_HEADS * RET_DK, GROUP_W, GROUP_W))
        ret_out = retention_mixer(rq.reshape(bsz, seq, RET_HEADS, RET_DK),
                                  rk.reshape(bsz, seq, RET_HEADS, RET_DK),
                                  rv.reshape(bsz, seq, RET_HEADS, RET_DV), cos, sin) * silu(rz)

        dq, dk, dv, dz = split_cols(diff_p, (GROUP_W, GROUP_W, GROUP_W, GROUP_W))
        lam_init = 0.8 - 0.6 * math.exp(-0.3 * l)
        lp = diff_lambda[l].astype(f32)
        lam = jnp.exp(jnp.sum(lp[0] * lp[1])) - jnp.exp(jnp.sum(lp[2] * lp[3])) + lam_init
        diff_out = diff_attention_mixer(dq.reshape(bsz, seq, DIFF_HEADS, 2, DIFF_DK),
                                        dk.reshape(bsz, seq, DIFF_HEADS, 2, DIFF_DK),
                                        dv.reshape(bsz, seq, DIFF_HEADS, DIFF_DV),
                                        lam, lam_init, diff_norm_w[l].astype(f32), cos, sin) * silu(dz)

        xbc, dt_raw, sz = split_cols(ssd_p, (SSD_XBC, SSD_HEADS, GROUP_W))
        y = ssd_mixer(xbc, dt_raw, ssd_conv_w[l].astype(f32), ssd_conv_b[l].astype(f32),
                      ssd_dt_bias[l].astype(f32), ssd_a_log[l].astype(f32), ssd_d[l].astype(f32))
        ssd_out = _rms(y * silu(sz)) * ssd_norm_w[l].astype(f32)

        mqk, mv, mo, mg, mz = split_cols(ml_p, (2 * GROUP_W, GROUP_W, GROUP_W, 2 * MLSTM_HEADS, GROUP_W))
        ml_out = mlstm_mixer(mqk, mv, mo, mg, mlstm_conv_w[l].astype(f32), mlstm_conv_b[l].astype(f32),
                             mlstm_gate_b[l].astype(f32), mlstm_norm_w[l].astype(f32)) * silu(mz)

        mix = jnp.concatenate([ret_out, diff_out, ssd_out, ml_out], axis=-1).astype(x.dtype)
        h = h + jnp.einsum('bse,ed->bsd', mix, w_out[l]).astype(x.dtype)
    return rmsnorm(h, final_norm_w).astype(x.dtype)
```

```python
import functools
import math

import numpy as np
import jax
import jax.numpy as jnp
from jax import lax
from jax.experimental import pallas as pl
from jax.experimental.pallas import tpu as pltpu

F32 = jnp.float32
MXU_DTYPE = jnp.bfloat16

D_MODEL = 1024
GROUP_W = 512
CHUNK = 128
ROPE_THETA = 10000.0
EPS = 1e-6
NEG_INF = -1e30

RET_HEADS, RET_DK, RET_DV = 4, 64, 128
DIFF_HEADS, DIFF_DK, DIFF_DV = 4, 64, 128
SSD_HEADS, SSD_HEAD_DIM, SSD_GROUPS, SSD_STATE, SSD_CONV = 8, 64, 2, 128, 4
SSD_XBC = GROUP_W + 2 * SSD_GROUPS * SSD_STATE
MLSTM_HEADS, MLSTM_DH, MLSTM_CONV = 4, 128, 4

RET_OFF = 0
DIFF_OFF = RET_OFF + 2 * RET_HEADS * RET_DK + 2 * GROUP_W
SSD_OFF = DIFF_OFF + 4 * GROUP_W
MLSTM_OFF = SSD_OFF + SSD_XBC + SSD_HEADS + GROUP_W

LANES = 128
CONV_PAD = 8
VMEM_LIMIT = 56 * 1024 * 1024

TOKEN_BLOCK = 512
ATTN_BLOCK = 256


def _dot(a, b):
    return jnp.dot(a.astype(MXU_DTYPE), b.astype(MXU_DTYPE), preferred_element_type=F32)


def _dot_nt(a, b):
    return lax.dot_general(a.astype(MXU_DTYPE), b.astype(MXU_DTYPE), (((1,), (1,)), ((), ())),
                           preferred_element_type=F32)


def _dot_tn(a, b):
    return lax.dot_general(a.astype(MXU_DTYPE), b.astype(MXU_DTYPE), (((0,), (0,)), ((), ())),
                           preferred_element_type=F32)


def _sigmoid(x):
    return 1.0 / (1.0 + jnp.exp(-x))


def _silu(x):
    return x * _sigmoid(x)


def _softplus(x):
    return jnp.maximum(x, 0.0) + jnp.log1p(jnp.exp(-jnp.abs(x)))


def _rms_lanes(x):
    return x * lax.rsqrt(jnp.mean(x * x, axis=-1, keepdims=True) + EPS)


def _rope(a, cos, sin):
    x1, x2 = a[:, :LANES], a[:, LANES:]
    return jnp.concatenate([x1 * cos - x2 * sin, x2 * cos + x1 * sin], axis=-1)


def _cumsum_rows(x, tri):
    hi = x.astype(jnp.bfloat16)
    r1 = x - hi.astype(F32)
    mid = r1.astype(jnp.bfloat16)
    lo = (r1 - mid.astype(F32)).astype(jnp.bfloat16)
    mm = lambda t: jnp.dot(tri, t, preferred_element_type=F32)
    return mm(hi) + mm(mid) + mm(lo)


def _tri(n):
    row = lax.broadcasted_iota(jnp.int32, (n, n), 0)
    col = lax.broadcasted_iota(jnp.int32, (n, n), 1)
    return col <= row


def _normed_input(x_ref, nw_ref):
    x = x_ref[0]
    return (_rms_lanes(x) * nw_ref[...]).astype(MXU_DTYPE)


def _causal_conv_silu(ext_ref, raw, conv_w_ref, conv_b_ref, first_block):
    ts = raw.shape[0]

    @pl.when(first_block)
    def _():
        ext_ref[0:CONV_PAD, :] = jnp.zeros((CONV_PAD, raw.shape[1]), F32)

    ext_ref[CONV_PAD:CONV_PAD + ts, :] = raw
    taps = conv_w_ref.shape[0]
    y = conv_b_ref[...]
    for j in range(taps):
        off = CONV_PAD - (taps - 1) + j
        y = y + conv_w_ref[j:j + 1, :] * ext_ref[off:off + ts, :]
    ext_ref[0:CONV_PAD, :] = ext_ref[ts:ts + CONV_PAD, :]
    return _silu(y)


def _diff_proj_kernel(x_ref, nw_ref, w_ref, cos_ref, sin_ref, q_ref, k_ref, v_ref, z_ref):
    u = _normed_input(x_ref, nw_ref)
    proj = jnp.dot(u, w_ref[...], preferred_element_type=F32)
    cos, sin = cos_ref[...], sin_ref[...]
    for p in range(2):
        lo = p * 256
        q_ref[0, :, lo:lo + 256] = (_rope(proj[:, lo:lo + 256], cos, sin) * (DIFF_DK ** -0.5)).astype(q_ref.dtype)
        k_ref[0, :, lo:lo + 256] = _rope(proj[:, 512 + lo:512 + lo + 256], cos, sin).astype(k_ref.dtype)
    v_ref[0] = proj[:, 1024:1536].astype(v_ref.dtype)
    z_ref[0] = _silu(proj[:, 1536:2048])


def _diff_flash_kernel(lam_init, q_ref, k_ref, v_ref, z_ref, lp_ref, nw_ref, o_ref,
                       qm_ref, m_ref, l_ref, acc_ref):
    tq = q_ref.shape[1]
    tk = tq
    i = pl.program_id(2)
    q = q_ref[0]
    gid = (lax.broadcasted_iota(jnp.int32, (1, 256), 1) % LANES) // 32
    for g in range(4):
        qm_ref[g * tq:(g + 1) * tq, :] = jnp.where(gid == g, q, jnp.zeros_like(q))
    m_ref[...] = jnp.full(m_ref.shape, NEG_INF, F32)
    l_ref[...] = jnp.zeros(l_ref.shape, F32)
    acc_ref[...] = jnp.zeros(acc_ref.shape, F32)

    def step(j, masked):
        r0 = pl.multiple_of(j * tk, tk)
        kj = k_ref[0, pl.ds(r0, tk), :]
        vj = v_ref[0, pl.ds(r0, tk), :]
        s = _dot_nt(qm_ref[...], kj)
        if masked:
            keep = _tri(tq)
        for g in range(4):
            sg = s[g * tq:(g + 1) * tq, :]
            if masked:
                sg = jnp.where(keep, sg, NEG_INF)
            m_prev = m_ref[g]
            m_new = jnp.maximum(m_prev, jnp.max(sg, axis=-1, keepdims=True))
            alpha = jnp.exp(m_prev - m_new)
            p = jnp.exp(sg - m_new[:, 0:1])
            l_ref[g] = alpha * l_ref[g] + jnp.sum(p, axis=-1, keepdims=True)
            hh = g // 2
            acc_ref[g] = alpha * acc_ref[g] + _dot(p, vj[:, hh * LANES:(hh + 1) * LANES])
            m_ref[g] = m_new

    def body(j, carry):
        step(j, False)
        return carry

    lax.fori_loop(0, i, body, 0)
    step(i, True)

    lp = lp_ref[...]
    lam = (jnp.exp(jnp.sum(lp[0:1] * lp[1:2], axis=-1, keepdims=True))
           - jnp.exp(jnp.sum(lp[2:3] * lp[3:4], axis=-1, keepdims=True)) + lam_init)
    for hh in range(2):
        o1 = acc_ref[2 * hh] / l_ref[2 * hh]
        o2 = acc_ref[2 * hh + 1] / l_ref[2 * hh + 1]
        y = _rms_lanes(o1 - lam * o2) * nw_ref[...] * (1.0 - lam_init)
        sl = slice(hh * LANES, (hh + 1) * LANES)
        o_ref[0, :, sl] = (y * z_ref[0, :, sl]).astype(o_ref.dtype)


def _ret_kernel(x_ref, nw_ref, w_ref, cos_ref, sin_ref, dec_ref, qw_ref, kw_ref, cd_ref, msk_ref,
                o_ref, st_ref, proj_ref):
    ts = x_ref.shape[1]

    @pl.when(pl.program_id(1) == 0)
    def _():
        st_ref[...] = jnp.zeros(st_ref.shape, F32)

    proj_ref[...] = jnp.dot(_normed_input(x_ref, nw_ref), w_ref[...], preferred_element_type=F32)
    hid = (lax.broadcasted_iota(jnp.int32, (1, 256), 1) % LANES) // 32

    def chunk(c, carry):
        r0 = pl.multiple_of(c * CHUNK, CHUNK)
        rows = pl.ds(r0, CHUNK)
        cos, sin = cos_ref[rows, :], sin_ref[rows, :]
        qr = _rope(proj_ref[rows, 0:256], cos, sin)
        kr = _rope(proj_ref[rows, 256:512], cos, sin) * (RET_DK ** -0.5)
        v = proj_ref[rows, 512:1024].astype(MXU_DTYPE)
        krb = kr.astype(MXU_DTYPE)
        state = st_ref[...]
        inter = _dot(qr * qw_ref[...], state)
        st_ref[...] = state * cd_ref[...] + _dot_tn(kr * kw_ref[...], v) * msk_ref[...]
        for h in range(RET_HEADS):
            sl = slice(h * RET_DV, (h + 1) * RET_DV)
            sc = _dot_nt(jnp.where(hid == h, qr, 0.0), krb) * dec_ref[h]
            y = _rms_lanes(_dot(sc, v[:, sl]) + inter[:, sl])
            z = proj_ref[rows, 1024 + h * RET_DV:1024 + (h + 1) * RET_DV]
            o_ref[0, rows, sl] = (y * _silu(z)).astype(o_ref.dtype)
        return carry

    lax.fori_loop(0, ts // CHUNK, chunk, 0)


def _ssd_kernel(x_ref, nw_ref, w_ref, cw_ref, cb_ref, dtb_ref, alog_ref, dskip_ref, onw_ref,
                o_ref, st_ref, ext_ref, act_ref, dt_ref, z_ref):
    ts = x_ref.shape[1]
    first = pl.program_id(1) == 0

    @pl.when(first)
    def _():
        st_ref[...] = jnp.zeros(st_ref.shape, F32)

    proj = jnp.dot(_normed_input(x_ref, nw_ref), w_ref[...], preferred_element_type=F32)
    act_ref[...] = _causal_conv_silu(ext_ref, proj[:, 0:SSD_XBC], cw_ref, cb_ref, first)
    dt_ref[...] = _softplus(proj[:, SSD_XBC:SSD_XBC + GROUP_W] + dtb_ref[...])
    z_ref[...] = _silu(proj[:, SSD_XBC + GROUP_W:SSD_XBC + 2 * GROUP_W])
    a_neg = -jnp.exp(alog_ref[...])
    tri = _tri(CHUNK)
    tri_b = tri.astype(jnp.bfloat16)
    lane = lax.broadcasted_iota(jnp.int32, (1, LANES), 1)
    gw = GROUP_W // SSD_GROUPS

    def chunk(c, carry):
        r0 = pl.multiple_of(c * CHUNK, CHUNK)
        rows = pl.ds(r0, CHUNK)
        xs = act_ref[rows, 0:GROUP_W]
        dt = dt_ref[rows, :]
        cs = _cumsum_rows(dt * a_neg, tri_b)
        cs_last = cs[CHUNK - 1:CHUNK, :]
        xdt = xs * dt
        y_parts = []
        for g in range(SSD_GROUPS):
            bm = act_ref[rows, GROUP_W + g * SSD_STATE:GROUP_W + (g + 1) * SSD_STATE]
            cm = act_ref[rows, GROUP_W + (SSD_GROUPS + g) * SSD_STATE:GROUP_W + (SSD_GROUPS + g + 1) * SSD_STATE]
            cb = _dot_nt(cm, bm)
            gs = slice(g * gw, (g + 1) * gw)
            prev = st_ref[g]
            y_off = _dot(cm, prev) * jnp.exp(cs[:, gs])
            st_ref[g] = prev * jnp.exp(cs_last[:, gs]) + _dot_tn(bm, xdt[:, gs] * jnp.exp(cs_last[:, gs] - cs[:, gs]))
            for pr in range(gw // LANES):
                ls = slice(g * gw + pr * LANES, g * gw + (pr + 1) * LANES)
                cs_t = cs[:, ls].T
                xdt_pair = xdt[:, ls]
                y_pair = None
                for e in range(2):
                    col = cs[:, ls][:, e * SSD_HEAD_DIM:e * SSD_HEAD_DIM + 1]
                    row = cs_t[e * SSD_HEAD_DIM:e * SSD_HEAD_DIM + 1, :]
                    lmat = jnp.exp(jnp.where(tri, col - row, -jnp.inf))
                    half = jnp.where((lane // SSD_HEAD_DIM) == e, xdt_pair, 0.0)
                    term = _dot(cb * lmat, half)
                    y_pair = term if y_pair is None else y_pair + term
                y_parts.append(y_pair + y_off[:, pr * LANES:(pr + 1) * LANES])
        y = jnp.concatenate(y_parts, axis=-1) + xs * dskip_ref[...]
        o_ref[0, rows, :] = (_rms_lanes(y * z_ref[rows, :]) * onw_ref[...]).astype(o_ref.dtype)
        return carry

    lax.fori_loop(0, ts // CHUNK, chunk, 0)


def _mlstm_kernel(x_ref, nw_ref, w_ref, cw_ref, cb_ref, ib_ref, fb_ref, onw_ref,
                  o_ref, ct_ref, n_ref, m_ref, ext_ref, act_ref, proj_ref):
    ts = x_ref.shape[1]
    first = pl.program_id(1) == 0
    nh, dh = MLSTM_HEADS, MLSTM_DH

    @pl.when(first)
    def _():
        ct_ref[...] = jnp.zeros(ct_ref.shape, F32)
        n_ref[...] = jnp.zeros(n_ref.shape, F32)
        m_ref[...] = jnp.zeros(m_ref.shape, F32)

    proj_ref[...] = jnp.dot(_normed_input(x_ref, nw_ref), w_ref[...], preferred_element_type=F32)
    act_ref[...] = _causal_conv_silu(ext_ref, proj_ref[:, 0:2 * GROUP_W], cw_ref, cb_ref, first)
    tri = _tri(CHUNK)
    tri_b = tri.astype(jnp.bfloat16)
    c_v, c_o, c_i, c_f, c_z = (2 * GROUP_W + k * GROUP_W for k in range(5))

    def chunk(c, carry):
        r0 = pl.multiple_of(c * CHUNK, CHUNK)
        rows = pl.ds(r0, CHUNK)
        ig = proj_ref[rows, c_i:c_i + GROUP_W] + ib_ref[...]
        fg = -_softplus(-(proj_ref[rows, c_f:c_f + GROUP_W] + fb_ref[...]))
        bcs = _cumsum_rows(fg, tri_b)
        for h in range(nh):
            sl = slice(h * dh, (h + 1) * dh)
            b_col = bcs[:, sl]
            b_row = b_col.T
            i_col = ig[:, sl]
            d_log = jnp.where(tri, b_col - b_row + i_col.T, -jnp.inf)
            m_prev = m_ref[h:h + 1, :]
            inter_log = b_col + m_prev
            m_row = jnp.maximum(jnp.max(d_log, axis=-1, keepdims=True), inter_log)
            qh = act_ref[rows, sl]
            kh = act_ref[rows, GROUP_W + h * dh:GROUP_W + (h + 1) * dh] * (dh ** -0.5)
            vh = proj_ref[rows, c_v + h * dh:c_v + (h + 1) * dh].astype(MXU_DTYPE)
            s = _dot_nt(qh, kh) * jnp.exp(d_log - m_row)
            inter_w = jnp.exp(inter_log - m_row)
            ct = ct_ref[h]
            n_row = n_ref[h:h + 1, :]
            num = _dot(s, vh) + inter_w * _dot(qh, ct)
            qn = jnp.sum(s, axis=-1, keepdims=True) + inter_w * jnp.sum(qh * n_row, axis=-1, keepdims=True)
            hv = num / jnp.maximum(jnp.abs(qn), jnp.exp(-m_row))
            b_last = b_col[CHUNK - 1:CHUNK, :]
            w_log = b_last - b_col + i_col
            m_new = jnp.maximum(b_last + m_prev, jnp.max(w_log, axis=0, keepdims=True))
            kw = kh * jnp.exp(w_log - m_new)
            decay = jnp.exp(b_last + m_prev - m_new)
            ct_ref[h] = decay * ct + _dot_tn(kw, vh)
            n_ref[h:h + 1, :] = decay * n_row + jnp.sum(kw, axis=0, keepdims=True)
            m_ref[h:h + 1, :] = m_new
            hv = _sigmoid(proj_ref[rows, c_o + h * dh:c_o + (h + 1) * dh]) * hv
            hv = _rms_lanes(hv - jnp.mean(hv, axis=-1, keepdims=True)) * onw_ref[:, sl]
            o_ref[0, rows, sl] = (hv * _silu(proj_ref[rows, c_z + h * dh:c_z + (h + 1) * dh])).astype(o_ref.dtype)
        return carry

    lax.fori_loop(0, ts // CHUNK, chunk, 0)


def _out_kernel(final_norm, h_ref, a_ref, b_ref, c_ref, d_ref, w_ref, fw_ref, o_ref):
    acc = h_ref[0]
    for k, m_ref in enumerate((a_ref, b_ref, c_ref, d_ref)):
        acc = acc + jnp.dot(m_ref[0], w_ref[k * GROUP_W:(k + 1) * GROUP_W, :], preferred_element_type=F32)
    if final_norm:
        acc = _rms_lanes(acc) * fw_ref[...]
    o_ref[0] = acc


def _params(*sem):
    return pltpu.CompilerParams(dimension_semantics=sem, vmem_limit_bytes=VMEM_LIMIT)


def _const_spec(shape):
    nd = len(shape)
    return pl.BlockSpec(shape, lambda *_: (0,) * nd)


def _rope_perm(groups):
    idx = []
    for blk in range(groups // 4):
        for e in range(2):
            for g4 in range(4):
                base = (blk * 4 + g4) * 64 + e * 32
                idx.extend(range(base, base + 32))
    return np.asarray(idx, np.int32)


def _retention_tables():
    nh, L = RET_HEADS, CHUNK
    log_g = jnp.log(1.0 - jnp.exp2(-5.0 - jnp.arange(nh, dtype=F32)))
    pos = jnp.arange(L, dtype=F32)
    rel = pos[:, None] - pos[None, :]
    decay = jnp.where(rel >= 0, jnp.exp(log_g[:, None, None] * jnp.maximum(rel, 0.0)), 0.0)
    head_of_col = (np.arange(256) % LANES) // 32
    q_w = jnp.exp(log_g[None, :] * (pos + 1.0)[:, None])[:, head_of_col]
    k_w = jnp.exp(log_g[None, :] * (L - 1.0 - pos)[:, None])[:, head_of_col]
    same_head = jnp.asarray(head_of_col[:, None] == (np.arange(GROUP_W) // RET_DV)[None, :], F32)
    chunk_decay = jnp.exp(log_g * L)[head_of_col][:, None] * same_head
    return decay, q_w, k_w, chunk_decay, same_head


def kernel(x, norm_w, w_in, w_out, diff_lambda, diff_norm_w, ssd_conv_w, ssd_conv_b, ssd_dt_bias, ssd_a_log,
           ssd_d, ssd_norm_w, mlstm_conv_w, mlstm_conv_b, mlstm_gate_b, mlstm_norm_w, final_norm_w):
    bsz, seq, dm = x.shape
    depth = w_in.shape[0]
    ts = min(TOKEN_BLOCK, seq)
    tq = min(ATTN_BLOCK, seq)
    nblk = seq // ts
    bf = MXU_DTYPE

    inv = ROPE_THETA ** (-jnp.arange(0, RET_DK, 2, dtype=F32) / RET_DK)
    ang = jnp.arange(seq, dtype=F32)[:, None] * inv[None, :]
    cos4, sin4 = jnp.tile(jnp.cos(ang), (1, 4)), jnp.tile(jnp.sin(ang), (1, 4))
    ret_tabs = _retention_tables()
    perm_ret, perm_diff = _rope_perm(RET_HEADS), _rope_perm(2 * DIFF_HEADS)

    x_spec = pl.BlockSpec((1, ts, dm), lambda b, s: (b, s, 0))
    row_spec = pl.BlockSpec((ts, LANES), lambda b, s: (s, 0))
    grp_spec = pl.BlockSpec((1, ts, GROUP_W), lambda b, s: (b, s, 0))
    grp_shape = jax.ShapeDtypeStruct((bsz, seq, GROUP_W), bf)

    h = x
    for l in range(depth):
        w = w_in[l]
        nw = norm_w[l].reshape(1, dm)
        rep = lambda a, n: jnp.repeat(a, n, axis=-1)

        w_diff = jnp.concatenate([w[:, DIFF_OFF:DIFF_OFF + 512][:, perm_diff],
                                  w[:, DIFF_OFF + 512:DIFF_OFF + 1024][:, perm_diff],
                                  w[:, DIFF_OFF + 1024:DIFF_OFF + 2048]], axis=1).astype(bf)
        dq, dk, dv, dz = pl.pallas_call(
            _diff_proj_kernel,
            grid=(bsz, nblk),
            in_specs=[x_spec, _const_spec((1, dm)), _const_spec(w_diff.shape), row_spec, row_spec],
            out_specs=[grp_spec] * 4,
            out_shape=[grp_shape, grp_shape, grp_shape, jax.ShapeDtypeStruct((bsz, seq, GROUP_W), F32)],
            compiler_params=_params("parallel", "parallel"),
            name="diff_proj",
        )(h, nw, w_diff, cos4, sin4)

        lam_init = 0.8 - 0.6 * math.exp(-0.3 * l)
        pair_q = pl.BlockSpec((1, tq, 256), lambda b, p, i: (b, i, p))
        pair_kv = pl.BlockSpec((1, seq, 256), lambda b, p, i: (b, 0, p))
        diff_out = pl.pallas_call(
            functools.partial(_diff_flash_kernel, lam_init),
            grid=(bsz, 2, seq // tq),
            in_specs=[pair_q, pair_kv, pair_kv, pair_q, _const_spec((4, DIFF_DK)), _const_spec((1, DIFF_DV))],
            out_specs=pair_q,
            out_shape=grp_shape,
            scratch_shapes=[pltpu.VMEM((4 * tq, 256), bf), pltpu.VMEM((4, tq, LANES), F32),
                            pltpu.VMEM((4, tq, LANES), F32), pltpu.VMEM((4, tq, DIFF_DV), F32)],
            compiler_params=_params("parallel", "parallel", "arbitrary"),
            name="diff_flash",
        )(dq, dk, dv, dz, diff_lambda[l].astype(F32), diff_norm_w[l].astype(F32).reshape(1, DIFF_DV))

        w_ret = jnp.concatenate([w[:, RET_OFF:RET_OFF + 256][:, perm_ret],
                                 w[:, RET_OFF + 256:RET_OFF + 512][:, perm_ret],
                                 w[:, RET_OFF + 512:RET_OFF + 1536]], axis=1).astype(bf)
        ret_out = pl.pallas_call(
            _ret_kernel,
            grid=(bsz, nblk),
            in_specs=[x_spec, _const_spec((1, dm)), _const_spec(w_ret.shape), row_spec, row_spec]
                     + [_const_spec(t.shape) for t in ret_tabs],
            out_specs=grp_spec,
            out_shape=grp_shape,
            scratch_shapes=[pltpu.VMEM((256, GROUP_W), F32), pltpu.VMEM((ts, w_ret.shape[1]), F32)],
            compiler_params=_params("parallel", "arbitrary"),
            name="retention",
        )(h, nw, w_ret, cos4, sin4, *ret_tabs)

        w_ssd = jnp.concatenate([w[:, SSD_OFF:SSD_OFF + SSD_XBC],
                                 rep(w[:, SSD_OFF + SSD_XBC:SSD_OFF + SSD_XBC + SSD_HEADS], SSD_HEAD_DIM),
                                 w[:, SSD_OFF + SSD_XBC + SSD_HEADS:MLSTM_OFF]], axis=1).astype(bf)
        head_row = lambda a: rep(a.astype(F32), SSD_HEAD_DIM).reshape(1, GROUP_W)
        ssd_out = pl.pallas_call(
            _ssd_kernel,
            grid=(bsz, nblk),
            in_specs=[x_spec, _const_spec((1, dm)), _const_spec(w_ssd.shape), _const_spec((SSD_CONV, SSD_XBC)),
                      _const_spec((1, SSD_XBC))] + [_const_spec((1, GROUP_W))] * 4,
            out_specs=grp_spec,
            out_shape=grp_shape,
            scratch_shapes=[pltpu.VMEM((SSD_GROUPS, SSD_STATE, GROUP_W // SSD_GROUPS), F32),
                            pltpu.VMEM((ts + CONV_PAD, SSD_XBC), F32), pltpu.VMEM((ts, SSD_XBC), F32),
                            pltpu.VMEM((ts, GROUP_W), F32), pltpu.VMEM((ts, GROUP_W), F32)],
            compiler_params=_params("parallel", "arbitrary"),
            name="ssd",
        )(h, nw, w_ssd, ssd_conv_w[l].astype(F32), ssd_conv_b[l].astype(F32).reshape(1, SSD_XBC),
          head_row(ssd_dt_bias[l]), head_row(ssd_a_log[l]), head_row(ssd_d[l]),
          ssd_norm_w[l].astype(F32).reshape(1, GROUP_W))

        g0 = MLSTM_OFF + 4 * GROUP_W
        w_ml = jnp.concatenate([w[:, MLSTM_OFF:g0],
                                rep(w[:, g0:g0 + MLSTM_HEADS], MLSTM_DH),
                                rep(w[:, g0 + MLSTM_HEADS:g0 + 2 * MLSTM_HEADS], MLSTM_DH),
                                w[:, g0 + 2 * MLSTM_HEADS:g0 + 2 * MLSTM_HEADS + GROUP_W]], axis=1).astype(bf)
        gate_b = mlstm_gate_b[l].astype(F32)
        ml_out = pl.pallas_call(
            _mlstm_kernel,
            grid=(bsz, nblk),
            in_specs=[x_spec, _const_spec((1, dm)), _const_spec(w_ml.shape),
                      _const_spec((MLSTM_CONV, 2 * GROUP_W)), _const_spec((1, 2 * GROUP_W))]
                     + [_const_spec((1, GROUP_W))] * 3,
            out_specs=grp_spec,
            out_shape=grp_shape,
            scratch_shapes=[pltpu.VMEM((MLSTM_HEADS, MLSTM_DH, MLSTM_DH), F32), pltpu.VMEM((8, MLSTM_DH), F32),
                            pltpu.VMEM((8, LANES), F32), pltpu.VMEM((ts + CONV_PAD, 2 * GROUP_W), F32),
                            pltpu.VMEM((ts, 2 * GROUP_W), F32), pltpu.VMEM((ts, w_ml.shape[1]), F32)],
            compiler_params=_params("parallel", "arbitrary"),
            name="mlstm",
        )(h, nw, w_ml, mlstm_conv_w[l].astype(F32), mlstm_conv_b[l].astype(F32).reshape(1, 2 * GROUP_W),
          rep(gate_b[:MLSTM_HEADS], MLSTM_DH).reshape(1, GROUP_W),
          rep(gate_b[MLSTM_HEADS:], MLSTM_DH).reshape(1, GROUP_W),
          mlstm_norm_w[l].astype(F32).reshape(1, GROUP_W))

        last = l == depth - 1
        h = pl.pallas_call(
            functools.partial(_out_kernel, last),
            grid=(bsz, nblk),
            in_specs=[x_spec] + [grp_spec] * 4 + [_const_spec((4 * GROUP_W, dm)), _const_spec((1, dm))],
            out_specs=x_spec,
            out_shape=jax.ShapeDtypeStruct((bsz, seq, dm), F32),
            compiler_params=_params("parallel", "parallel"),
            name="out_proj",
        )(h, ret_out, diff_out, ssd_out, ml_out, w_out[l].astype(bf), final_norm_w.astype(F32).reshape(1, dm))
    return h
```

```python
import functools
import math

import numpy as np
import jax
import jax.numpy as jnp
from jax import lax
from jax.experimental import pallas as pl
from jax.experimental.pallas import tpu as pltpu

F32 = jnp.float32
MXU_DTYPE = jnp.bfloat16

D_MODEL = 1024
GROUP_W = 512
CHUNK = 128
ROPE_THETA = 10000.0
EPS = 1e-6
NEG_INF = -1e30
LOG2E = math.log2(math.e)

RET_HEADS, RET_DK, RET_DV = 4, 64, 128
DIFF_HEADS, DIFF_DK, DIFF_DV = 4, 64, 128
SSD_HEADS, SSD_HEAD_DIM, SSD_GROUPS, SSD_STATE, SSD_CONV = 8, 64, 2, 128, 4
SSD_XBC = GROUP_W + 2 * SSD_GROUPS * SSD_STATE
MLSTM_HEADS, MLSTM_DH, MLSTM_CONV = 4, 128, 4

RET_OFF = 0
DIFF_OFF = RET_OFF + 2 * RET_HEADS * RET_DK + 2 * GROUP_W
SSD_OFF = DIFF_OFF + 4 * GROUP_W
MLSTM_OFF = SSD_OFF + SSD_XBC + SSD_HEADS + GROUP_W

LANES = 128
CONV_PAD = 8
VMEM_LIMIT = 56 * 1024 * 1024

TOKEN_BLOCK = 512


def _dot(a, b):
    return jnp.dot(a.astype(MXU_DTYPE), b.astype(MXU_DTYPE), preferred_element_type=F32)


def _dot_nt(a, b):
    return lax.dot_general(a.astype(MXU_DTYPE), b.astype(MXU_DTYPE), (((1,), (1,)), ((), ())),
                           preferred_element_type=F32)


def _dot_tn(a, b):
    return lax.dot_general(a.astype(MXU_DTYPE), b.astype(MXU_DTYPE), (((0,), (0,)), ((), ())),
                           preferred_element_type=F32)


def _sigmoid(x):
    return 1.0 / (1.0 + jnp.exp(-x))


def _silu(x):
    return x * _sigmoid(x)


def _softplus(x):
    return jnp.maximum(x, 0.0) + jnp.log1p(jnp.exp(-jnp.abs(x)))


def _rms_lanes(x):
    return x * lax.rsqrt(jnp.mean(x * x, axis=-1, keepdims=True) + EPS)


def _rope(a, cos, sin):
    x1, x2 = a[:, :LANES], a[:, LANES:]
    return jnp.concatenate([x1 * cos - x2 * sin, x2 * cos + x1 * sin], axis=-1)


def _cumsum_rows(x, tri):
    hi = x.astype(jnp.bfloat16)
    r1 = x - hi.astype(F32)
    mid = r1.astype(jnp.bfloat16)
    lo = (r1 - mid.astype(F32)).astype(jnp.bfloat16)
    mm = lambda t: jnp.dot(tri, t, preferred_element_type=F32)
    return mm(hi) + mm(mid) + mm(lo)


def _tri(n):
    row = lax.broadcasted_iota(jnp.int32, (n, n), 0)
    col = lax.broadcasted_iota(jnp.int32, (n, n), 1)
    return col <= row


def _normed_input(x_ref, nw_ref):
    x = x_ref[0]
    return (_rms_lanes(x) * nw_ref[...]).astype(MXU_DTYPE)


def _causal_conv_silu(ext_ref, raw, conv_w_ref, conv_b_ref, first_block):
    ts = raw.shape[0]

    @pl.when(first_block)
    def _():
        ext_ref[0:CONV_PAD, :] = jnp.zeros((CONV_PAD, raw.shape[1]), F32)

    ext_ref[CONV_PAD:CONV_PAD + ts, :] = raw
    taps = conv_w_ref.shape[0]
    y = conv_b_ref[...]
    for j in range(taps):
        off = CONV_PAD - (taps - 1) + j
        y = y + conv_w_ref[j:j + 1, :] * ext_ref[off:off + ts, :]
    ext_ref[0:CONV_PAD, :] = ext_ref[ts:ts + CONV_PAD, :]
    return _silu(y)


def _diff_proj_kernel(x_ref, nw_ref, w_ref, cos_ref, sin_ref, q_ref, kt_ref, v_ref, z_ref):
    u = _normed_input(x_ref, nw_ref)
    proj = jnp.dot(u, w_ref[...], preferred_element_type=F32)
    cos, sin = cos_ref[...], sin_ref[...]
    ones = jnp.ones((proj.shape[0], DIFF_DV), v_ref.dtype)
    for p in range(2):
        lo = p * 256
        q = _rope(proj[:, lo:lo + 256], cos, sin) * (DIFF_DK ** -0.5 * LOG2E)
        q_ref[0, :, lo:lo + 256] = q.astype(q_ref.dtype)
        kt_ref[0, p, 0] = _rope(proj[:, 512 + lo:512 + lo + 256], cos, sin).T.astype(kt_ref.dtype)
    for h in range(DIFF_HEADS):
        v_ref[0, :, 2 * h * DIFF_DV:(2 * h + 1) * DIFF_DV] = (
            proj[:, 1024 + h * DIFF_DV:1024 + (h + 1) * DIFF_DV].astype(v_ref.dtype))
        v_ref[0, :, (2 * h + 1) * DIFF_DV:(2 * h + 2) * DIFF_DV] = ones
    z_ref[0] = _silu(proj[:, 1536:2048])


def _diff_flash_kernel(lam_init, q_ref, kt_ref, v_ref, z_ref, lp_ref, nw_ref, o_ref, qm_ref, m_ref, acc_ref):
    tq = q_ref.shape[1]
    tk = kt_ref.shape[-1]
    i = pl.program_id(2)
    q = q_ref[0]
    gid = (lax.broadcasted_iota(jnp.int32, (1, 256), 1) % LANES) // 32
    for g in range(4):
        qm_ref[g * tq:(g + 1) * tq, :] = jnp.where(gid == g, q, jnp.zeros_like(q))
    m_ref[...] = jnp.full(m_ref.shape, NEG_INF, F32)
    acc_ref[...] = jnp.zeros(acc_ref.shape, F32)

    def step(j, masked):
        r0 = pl.multiple_of(j * tk, tk)
        s = jnp.dot(qm_ref[...], kt_ref[0, 0, j], preferred_element_type=F32)
        if masked:
            keep = _tri(tq)
        for g in range(4):
            sg = s[g * tq:(g + 1) * tq, :]
            if masked:
                sg = jnp.where(keep, sg, NEG_INF)
            m_prev = m_ref[g]
            m_new = jnp.maximum(m_prev, jnp.max(sg, axis=-1, keepdims=True))
            alpha = jnp.exp2(m_prev - m_new)
            p = jnp.exp2(sg - jnp.concatenate([m_new] * (tk // LANES), axis=-1))
            hh = g // 2
            pv = _dot(p, v_ref[0, pl.ds(r0, tk), 2 * hh * DIFF_DV:(2 * hh + 2) * DIFF_DV])
            acc_ref[g] = jnp.concatenate([alpha, alpha], axis=-1) * acc_ref[g] + pv
            m_ref[g] = m_new

    def body(j, carry):
        step(j, False)
        return carry

    lax.fori_loop(0, i, body, 0)
    step(i, True)

    lp = lp_ref[...]
    lam = (jnp.exp(jnp.sum(lp[0:1] * lp[1:2], axis=-1, keepdims=True))
           - jnp.exp(jnp.sum(lp[2:3] * lp[3:4], axis=-1, keepdims=True)) + lam_init)
    for hh in range(2):
        a1, a2 = acc_ref[2 * hh], acc_ref[2 * hh + 1]
        o1 = a1[:, :DIFF_DV] / a1[:, DIFF_DV:]
        o2 = a2[:, :DIFF_DV] / a2[:, DIFF_DV:]
        y = _rms_lanes(o1 - lam * o2) * nw_ref[...] * (1.0 - lam_init)
        sl = slice(hh * LANES, (hh + 1) * LANES)
        o_ref[0, :, sl] = (y * z_ref[0, :, sl]).astype(o_ref.dtype)


def _ret_kernel(x_ref, nw_ref, w_ref, cos_ref, sin_ref, dec_ref, qw_ref, kw_ref, cd_ref, msk_ref,
                o_ref, st_ref, proj_ref):
    ts = x_ref.shape[1]

    @pl.when(pl.program_id(1) == 0)
    def _():
        st_ref[...] = jnp.zeros(st_ref.shape, F32)

    proj_ref[...] = jnp.dot(_normed_input(x_ref, nw_ref), w_ref[...], preferred_element_type=F32)
    hid = (lax.broadcasted_iota(jnp.int32, (1, 256), 1) % LANES) // 32

    def chunk(c, carry):
        r0 = pl.multiple_of(c * CHUNK, CHUNK)
        rows = pl.ds(r0, CHUNK)
        cos, sin = cos_ref[rows, :], sin_ref[rows, :]
        qr = _rope(proj_ref[rows, 0:256], cos, sin)
        kr = _rope(proj_ref[rows, 256:512], cos, sin) * (RET_DK ** -0.5)
        v = proj_ref[rows, 512:1024].astype(MXU_DTYPE)
        krb = kr.astype(MXU_DTYPE)
        state = st_ref[...]
        inter = _dot(qr * qw_ref[...], state)
        st_ref[...] = state * cd_ref[...] + _dot_tn(kr * kw_ref[...], v) * msk_ref[...]
        for h in range(RET_HEADS):
            sl = slice(h * RET_DV, (h + 1) * RET_DV)
            sc = _dot_nt(jnp.where(hid == h, qr, 0.0), krb) * dec_ref[h]
            y = _rms_lanes(_dot(sc, v[:, sl]) + inter[:, sl])
            z = proj_ref[rows, 1024 + h * RET_DV:1024 + (h + 1) * RET_DV]
            o_ref[0, rows, sl] = (y * _silu(z)).astype(o_ref.dtype)
        return carry

    lax.fori_loop(0, ts // CHUNK, chunk, 0)


def _ssd_kernel(x_ref, nw_ref, w_ref, cw_ref, cb_ref, dtb_ref, alog_ref, dskip_ref, onw_ref,
                o_ref, st_ref, ext_ref, act_ref, dt_ref, z_ref):
    ts = x_ref.shape[1]
    first = pl.program_id(1) == 0

    @pl.when(first)
    def _():
        st_ref[...] = jnp.zeros(st_ref.shape, F32)

    proj = jnp.dot(_normed_input(x_ref, nw_ref), w_ref[...], preferred_element_type=F32)
    act_ref[...] = _causal_conv_silu(ext_ref, proj[:, 0:SSD_XBC], cw_ref, cb_ref, first)
    dt_ref[...] = _softplus(proj[:, SSD_XBC:SSD_XBC + GROUP_W] + dtb_ref[...])
    z_ref[...] = _silu(proj[:, SSD_XBC + GROUP_W:SSD_XBC + 2 * GROUP_W])
    a_neg = -jnp.exp(alog_ref[...])
    tri = _tri(CHUNK)
    tri_b = tri.astype(jnp.bfloat16)
    lane = lax.broadcasted_iota(jnp.int32, (1, LANES), 1)
    gw = GROUP_W // SSD_GROUPS

    def chunk(c, carry):
        r0 = pl.multiple_of(c * CHUNK, CHUNK)
        rows = pl.ds(r0, CHUNK)
        xs = act_ref[rows, 0:GROUP_W]
        dt = dt_ref[rows, :]
        cs = _cumsum_rows(dt * a_neg, tri_b)
        cs_last = cs[CHUNK - 1:CHUNK, :]
        xdt = xs * dt
        y_parts = []
        for g in range(SSD_GROUPS):
            bm = act_ref[rows, GROUP_W + g * SSD_STATE:GROUP_W + (g + 1) * SSD_STATE]
            cm = act_ref[rows, GROUP_W + (SSD_GROUPS + g) * SSD_STATE:GROUP_W + (SSD_GROUPS + g + 1) * SSD_STATE]
            cb = _dot_nt(cm, bm)
            gs = slice(g * gw, (g + 1) * gw)
            prev = st_ref[g]
            y_off = _dot(cm, prev) * jnp.exp(cs[:, gs])
            st_ref[g] = prev * jnp.exp(cs_last[:, gs]) + _dot_tn(bm, xdt[:, gs] * jnp.exp(cs_last[:, gs] - cs[:, gs]))
            for pr in range(gw // LANES):
                ls = slice(g * gw + pr * LANES, g * gw + (pr + 1) * LANES)
                cs_t = cs[:, ls].T
                xdt_pair = xdt[:, ls]
                y_pair = None
                for e in range(2):
                    col = cs[:, ls][:, e * SSD_HEAD_DIM:e * SSD_HEAD_DIM + 1]
                    row = cs_t[e * SSD_HEAD_DIM:e * SSD_HEAD_DIM + 1, :]
                    lmat = jnp.exp(jnp.where(tri, col - row, -jnp.inf))
                    half = jnp.where((lane // SSD_HEAD_DIM) == e, xdt_pair, 0.0)
                    term = _dot(cb * lmat, half)
                    y_pair = term if y_pair is None else y_pair + term
                y_parts.append(y_pair + y_off[:, pr * LANES:(pr + 1) * LANES])
        y = jnp.concatenate(y_parts, axis=-1) + xs * dskip_ref[...]
        o_ref[0, rows, :] = (_rms_lanes(y * z_ref[rows, :]) * onw_ref[...]).astype(o_ref.dtype)
        return carry

    lax.fori_loop(0, ts // CHUNK, chunk, 0)


def _mlstm_kernel(x_ref, nw_ref, w_ref, cw_ref, cb_ref, ib_ref, fb_ref, onw_ref,
                  o_ref, ct_ref, n_ref, m_ref, ext_ref, act_ref, proj_ref):
    ts = x_ref.shape[1]
    first = pl.program_id(1) == 0
    nh, dh = MLSTM_HEADS, MLSTM_DH

    @pl.when(first)
    def _():
        ct_ref[...] = jnp.zeros(ct_ref.shape, F32)
        n_ref[...] = jnp.zeros(n_ref.shape, F32)
        m_ref[...] = jnp.zeros(m_ref.shape, F32)

    proj_ref[...] = jnp.dot(_normed_input(x_ref, nw_ref), w_ref[...], preferred_element_type=F32)
    act_ref[...] = _causal_conv_silu(ext_ref, proj_ref[:, 0:2 * GROUP_W], cw_ref, cb_ref, first)
    tri = _tri(CHUNK)
    tri_b = tri.astype(jnp.bfloat16)
    c_v, c_o, c_i, c_f, c_z = (2 * GROUP_W + k * GROUP_W for k in range(5))

    def chunk(c, carry):
        r0 = pl.multiple_of(c * CHUNK, CHUNK)
        rows = pl.ds(r0, CHUNK)
        ig = proj_ref[rows, c_i:c_i + GROUP_W] + ib_ref[...]
        fg = -_softplus(-(proj_ref[rows, c_f:c_f + GROUP_W] + fb_ref[...]))
        bcs = _cumsum_rows(fg, tri_b)
        for h in range(nh):
            sl = slice(h * dh, (h + 1) * dh)
            b_col = bcs[:, sl]
            b_row = b_col.T
            i_col = ig[:, sl]
            d_log = jnp.where(tri, b_col - b_row + i_col.T, -jnp.inf)
            m_prev = m_ref[h:h + 1, :]
            inter_log = b_col + m_prev
            m_row = jnp.maximum(jnp.max(d_log, axis=-1, keepdims=True), inter_log)
            qh = act_ref[rows, sl]
            kh = act_ref[rows, GROUP_W + h * dh:GROUP_W + (h + 1) * dh] * (dh ** -0.5)
            vh = proj_ref[rows, c_v + h * dh:c_v + (h + 1) * dh].astype(MXU_DTYPE)
            s = _dot_nt(qh, kh) * jnp.exp(d_log - m_row)
            inter_w = jnp.exp(inter_log - m_row)
            ct = ct_ref[h]
            n_row = n_ref[h:h + 1, :]
            num = _dot(s, vh) + inter_w * _dot(qh, ct)
            qn = jnp.sum(s, axis=-1, keepdims=True) + inter_w * jnp.sum(qh * n_row, axis=-1, keepdims=True)
            hv = num / jnp.maximum(jnp.abs(qn), jnp.exp(-m_row))
            b_last = b_col[CHUNK - 1:CHUNK, :]
            w_log = b_last - b_col + i_col
            m_new = jnp.maximum(b_last + m_prev, jnp.max(w_log, axis=0, keepdims=True))
            kw = kh * jnp.exp(w_log - m_new)
            decay = jnp.exp(b_last + m_prev - m_new)
            ct_ref[h] = decay * ct + _dot_tn(kw, vh)
            n_ref[h:h + 1, :] = decay * n_row + jnp.sum(kw, axis=0, keepdims=True)
            m_ref[h:h + 1, :] = m_new
            hv = _sigmoid(proj_ref[rows, c_o + h * dh:c_o + (h + 1) * dh]) * hv
            hv = _rms_lanes(hv - jnp.mean(hv, axis=-1, keepdims=True)) * onw_ref[:, sl]
            o_ref[0, rows, sl] = (hv * _silu(proj_ref[rows, c_z + h * dh:c_z + (h + 1) * dh])).astype(o_ref.dtype)
        return carry

    lax.fori_loop(0, ts // CHUNK, chunk, 0)


def _out_kernel(final_norm, h_ref, a_ref, b_ref, c_ref, d_ref, w_ref, fw_ref, o_ref):
    acc = h_ref[0]
    for k, m_ref in enumerate((a_ref, b_ref, c_ref, d_ref)):
        acc = acc + jnp.dot(m_ref[0], w_ref[k * GROUP_W:(k + 1) * GROUP_W, :], preferred_element_type=F32)
    if final_norm:
        acc = _rms_lanes(acc) * fw_ref[...]
    o_ref[0] = acc


def _params(*sem):
    return pltpu.CompilerParams(dimension_semantics=sem, vmem_limit_bytes=VMEM_LIMIT)


def _const_spec(shape):
    nd = len(shape)
    return pl.BlockSpec(shape, lambda *_: (0,) * nd)


def _rope_perm(groups):
    idx = []
    for blk in range(groups // 4):
        for e in range(2):
            for g4 in range(4):
                base = (blk * 4 + g4) * 64 + e * 32
                idx.extend(range(base, base + 32))
    return np.asarray(idx, np.int32)


def _retention_tables():
    nh, L = RET_HEADS, CHUNK
    log_g = jnp.log(1.0 - jnp.exp2(-5.0 - jnp.arange(nh, dtype=F32)))
    pos = jnp.arange(L, dtype=F32)
    rel = pos[:, None] - pos[None, :]
    decay = jnp.where(rel >= 0, jnp.exp(log_g[:, None, None] * jnp.maximum(rel, 0.0)), 0.0)
    head_of_col = (np.arange(256) % LANES) // 32
    q_w = jnp.exp(log_g[None, :] * (pos + 1.0)[:, None])[:, head_of_col]
    k_w = jnp.exp(log_g[None, :] * (L - 1.0 - pos)[:, None])[:, head_of_col]
    same_head = jnp.asarray(head_of_col[:, None] == (np.arange(GROUP_W) // RET_DV)[None, :], F32)
    chunk_decay = jnp.exp(log_g * L)[head_of_col][:, None] * same_head
    return decay, q_w, k_w, chunk_decay, same_head


def kernel(x, norm_w, w_in, w_out, diff_lambda, diff_norm_w, ssd_conv_w, ssd_conv_b, ssd_dt_bias, ssd_a_log,
           ssd_d, ssd_norm_w, mlstm_conv_w, mlstm_conv_b, mlstm_gate_b, mlstm_norm_w, final_norm_w):
    bsz, seq, dm = x.shape
    depth = w_in.shape[0]
    ts = min(TOKEN_BLOCK, seq)
    tq = ts
    nblk = seq // ts
    bf = MXU_DTYPE

    inv = ROPE_THETA ** (-jnp.arange(0, RET_DK, 2, dtype=F32) / RET_DK)
    ang = jnp.arange(seq, dtype=F32)[:, None] * inv[None, :]
    cos4, sin4 = jnp.tile(jnp.cos(ang), (1, 4)), jnp.tile(jnp.sin(ang), (1, 4))
    ret_tabs = _retention_tables()
    perm_ret, perm_diff = _rope_perm(RET_HEADS), _rope_perm(2 * DIFF_HEADS)

    x_spec = pl.BlockSpec((1, ts, dm), lambda b, s: (b, s, 0))
    row_spec = pl.BlockSpec((ts, LANES), lambda b, s: (s, 0))
    grp_spec = pl.BlockSpec((1, ts, GROUP_W), lambda b, s: (b, s, 0))
    grp_shape = jax.ShapeDtypeStruct((bsz, seq, GROUP_W), bf)

    h = x
    for l in range(depth):
        w = w_in[l]
        nw = norm_w[l].reshape(1, dm)
        rep = lambda a, n: jnp.repeat(a, n, axis=-1)

        w_diff = jnp.concatenate([w[:, DIFF_OFF:DIFF_OFF + 512][:, perm_diff],
                                  w[:, DIFF_OFF + 512:DIFF_OFF + 1024][:, perm_diff],
                                  w[:, DIFF_OFF + 1024:DIFF_OFF + 2048]], axis=1).astype(bf)
        nkb = seq // tq
        dq, dkt, dv, dz = pl.pallas_call(
            _diff_proj_kernel,
            grid=(bsz, nblk),
            in_specs=[x_spec, _const_spec((1, dm)), _const_spec(w_diff.shape), row_spec, row_spec],
            out_specs=[grp_spec,
                       pl.BlockSpec((1, 2, 1, 256, ts), lambda b, s: (b, 0, s, 0, 0)),
                       pl.BlockSpec((1, ts, 2 * GROUP_W), lambda b, s: (b, s, 0)),
                       grp_spec],
            out_shape=[grp_shape, jax.ShapeDtypeStruct((bsz, 2, nkb, 256, ts), bf),
                       jax.ShapeDtypeStruct((bsz, seq, 2 * GROUP_W), bf),
                       jax.ShapeDtypeStruct((bsz, seq, GROUP_W), F32)],
            compiler_params=_params("parallel", "parallel"),
            name="diff_proj",
        )(h, nw, w_diff, cos4, sin4)

        lam_init = 0.8 - 0.6 * math.exp(-0.3 * l)
        pair_q = pl.BlockSpec((1, tq, 256), lambda b, p, i: (b, i, p))
        diff_out = pl.pallas_call(
            functools.partial(_diff_flash_kernel, lam_init),
            grid=(bsz, 2, seq // tq),
            in_specs=[pair_q,
                      pl.BlockSpec((1, 1, nkb, 256, tq), lambda b, p, i: (b, p, 0, 0, 0)),
                      pl.BlockSpec((1, seq, GROUP_W), lambda b, p, i: (b, 0, p)),
                      pair_q, _const_spec((4, DIFF_DK)), _const_spec((1, DIFF_DV))],
            out_specs=pair_q,
            out_shape=grp_shape,
            scratch_shapes=[pltpu.VMEM((4 * tq, 256), bf), pltpu.VMEM((4, tq, LANES), F32),
                            pltpu.VMEM((4, tq, 2 * DIFF_DV), F32)],
            compiler_params=_params("parallel", "parallel", "arbitrary"),
            name="diff_flash",
        )(dq, dkt, dv, dz, diff_lambda[l].astype(F32), diff_norm_w[l].astype(F32).reshape(1, DIFF_DV))

        w_ret = jnp.concatenate([w[:, RET_OFF:RET_OFF + 256][:, perm_ret],
                                 w[:, RET_OFF + 256:RET_OFF + 512][:, perm_ret],
                                 w[:, RET_OFF + 512:RET_OFF + 1536]], axis=1).astype(bf)
        ret_out = pl.pallas_call(
            _ret_kernel,
            grid=(bsz, nblk),
            in_specs=[x_spec, _const_spec((1, dm)), _const_spec(w_ret.shape), row_spec, row_spec]
                     + [_const_spec(t.shape) for t in ret_tabs],
            out_specs=grp_spec,
            out_shape=grp_shape,
            scratch_shapes=[pltpu.VMEM((256, GROUP_W), F32), pltpu.VMEM((ts, w_ret.shape[1]), F32)],
            compiler_params=_params("parallel", "arbitrary"),
            name="retention",
        )(h, nw, w_ret, cos4, sin4, *ret_tabs)

        w_ssd = jnp.concatenate([w[:, SSD_OFF:SSD_OFF + SSD_XBC],
                                 rep(w[:, SSD_OFF + SSD_XBC:SSD_OFF + SSD_XBC + SSD_HEADS], SSD_HEAD_DIM),
                                 w[:, SSD_OFF + SSD_XBC + SSD_HEADS:MLSTM_OFF]], axis=1).astype(bf)
        head_row = lambda a: rep(a.astype(F32), SSD_HEAD_DIM).reshape(1, GROUP_W)
        ssd_out = pl.pallas_call(
            _ssd_kernel,
            grid=(bsz, nblk),
            in_specs=[x_spec, _const_spec((1, dm)), _const_spec(w_ssd.shape), _const_spec((SSD_CONV, SSD_XBC)),
                      _const_spec((1, SSD_XBC))] + [_const_spec((1, GROUP_W))] * 4,
            out_specs=grp_spec,
            out_shape=grp_shape,
            scratch_shapes=[pltpu.VMEM((SSD_GROUPS, SSD_STATE, GROUP_W // SSD_GROUPS), F32),
                            pltpu.VMEM((ts + CONV_PAD, SSD_XBC), F32), pltpu.VMEM((ts, SSD_XBC), F32),
                            pltpu.VMEM((ts, GROUP_W), F32), pltpu.VMEM((ts, GROUP_W), F32)],
            compiler_params=_params("parallel", "arbitrary"),
            name="ssd",
        )(h, nw, w_ssd, ssd_conv_w[l].astype(F32), ssd_conv_b[l].astype(F32).reshape(1, SSD_XBC),
          head_row(ssd_dt_bias[l]), head_row(ssd_a_log[l]), head_row(ssd_d[l]),
          ssd_norm_w[l].astype(F32).reshape(1, GROUP_W))

        g0 = MLSTM_OFF + 4 * GROUP_W
        w_ml = jnp.concatenate([w[:, MLSTM_OFF:g0],
                                rep(w[:, g0:g0 + MLSTM_HEADS], MLSTM_DH),
                                rep(w[:, g0 + MLSTM_HEADS:g0 + 2 * MLSTM_HEADS], MLSTM_DH),
                                w[:, g0 + 2 * MLSTM_HEADS:g0 + 2 * MLSTM_HEADS + GROUP_W]], axis=1).astype(bf)
        gate_b = mlstm_gate_b[l].astype(F32)
        ml_out = pl.pallas_call(
            _mlstm_kernel,
            grid=(bsz, nblk),
            in_specs=[x_spec, _const_spec((1, dm)), _const_spec(w_ml.shape),
                      _const_spec((MLSTM_CONV, 2 * GROUP_W)), _const_spec((1, 2 * GROUP_W))]
                     + [_const_spec((1, GROUP_W))] * 3,
            out_specs=grp_spec,
            out_shape=grp_shape,
            scratch_shapes=[pltpu.VMEM((MLSTM_HEADS, MLSTM_DH, MLSTM_DH), F32), pltpu.VMEM((8, MLSTM_DH), F32),
                            pltpu.VMEM((8, LANES), F32), pltpu.VMEM((ts + CONV_PAD, 2 * GROUP_W), F32),
                            pltpu.VMEM((ts, 2 * GROUP_W), F32), pltpu.VMEM((ts, w_ml.shape[1]), F32)],
            compiler_params=_params("parallel", "arbitrary"),
            name="mlstm",
        )(h, nw, w_ml, mlstm_conv_w[l].astype(F32), mlstm_conv_b[l].astype(F32).reshape(1, 2 * GROUP_W),
          rep(gate_b[:MLSTM_HEADS], MLSTM_DH).reshape(1, GROUP_W),
          rep(gate_b[MLSTM_HEADS:], MLSTM_DH).reshape(1, GROUP_W),
          mlstm_norm_w[l].astype(F32).reshape(1, GROUP_W))

        last = l == depth - 1
        h = pl.pallas_call(
            functools.partial(_out_kernel, last),
            grid=(bsz, nblk),
            in_specs=[x_spec] + [grp_spec] * 4 + [_const_spec((4 * GROUP_W, dm)), _const_spec((1, dm))],
            out_specs=x_spec,
            out_shape=jax.ShapeDtypeStruct((bsz, seq, dm), F32),
            compiler_params=_params("parallel", "parallel"),
            name="out_proj",
        )(h, ret_out, diff_out, ssd_out, ml_out, w_out[l].astype(bf), final_norm_w.astype(F32).reshape(1, dm))
    return h
```

```python
import functools
import math

import numpy as np
import jax
import jax.numpy as jnp
from jax import lax
from jax.experimental import pallas as pl
from jax.experimental.pallas import tpu as pltpu

F32 = jnp.float32
MXU_DTYPE = jnp.bfloat16

D_MODEL = 1024
GROUP_W = 512
CHUNK = 128
ROPE_THETA = 10000.0
EPS = 1e-6
NEG_INF = -1e30
LOG2E = math.log2(math.e)

RET_HEADS, RET_DK, RET_DV = 4, 64, 128
DIFF_HEADS, DIFF_DK, DIFF_DV = 4, 64, 128
SSD_HEADS, SSD_HEAD_DIM, SSD_GROUPS, SSD_STATE, SSD_CONV = 8, 64, 2, 128, 4
SSD_XBC = GROUP_W + 2 * SSD_GROUPS * SSD_STATE
MLSTM_HEADS, MLSTM_DH, MLSTM_CONV = 4, 128, 4

RET_OFF = 0
DIFF_OFF = RET_OFF + 2 * RET_HEADS * RET_DK + 2 * GROUP_W
SSD_OFF = DIFF_OFF + 4 * GROUP_W
MLSTM_OFF = SSD_OFF + SSD_XBC + SSD_HEADS + GROUP_W

LANES = 128
CONV_PAD = 8
VT_ROWS = DIFF_DV + 16
VMEM_LIMIT = 56 * 1024 * 1024

TOKEN_BLOCK = 512


def _dot(a, b):
    return jnp.dot(a.astype(MXU_DTYPE), b.astype(MXU_DTYPE), preferred_element_type=F32)


def _dot_nt(a, b):
    return lax.dot_general(a.astype(MXU_DTYPE), b.astype(MXU_DTYPE), (((1,), (1,)), ((), ())),
                           preferred_element_type=F32)


def _dot_tn(a, b):
    return lax.dot_general(a.astype(MXU_DTYPE), b.astype(MXU_DTYPE), (((0,), (0,)), ((), ())),
                           preferred_element_type=F32)


def _sigmoid(x):
    return 1.0 / (1.0 + jnp.exp(-x))


def _silu(x):
    return x * _sigmoid(x)


def _softplus(x):
    return jnp.maximum(x, 0.0) + jnp.log1p(jnp.exp(-jnp.abs(x)))


def _rms_lanes(x):
    return x * lax.rsqrt(jnp.mean(x * x, axis=-1, keepdims=True) + EPS)


def _rope(a, cos, sin):
    x1, x2 = a[:, :LANES], a[:, LANES:]
    return jnp.concatenate([x1 * cos - x2 * sin, x2 * cos + x1 * sin], axis=-1)


def _cumsum_rows(x, tri):
    hi = x.astype(jnp.bfloat16)
    r1 = x - hi.astype(F32)
    mid = r1.astype(jnp.bfloat16)
    lo = (r1 - mid.astype(F32)).astype(jnp.bfloat16)
    mm = lambda t: jnp.dot(tri, t, preferred_element_type=F32)
    return mm(hi) + mm(mid) + mm(lo)


def _tri(n):
    row = lax.broadcasted_iota(jnp.int32, (n, n), 0)
    col = lax.broadcasted_iota(jnp.int32, (n, n), 1)
    return col <= row


def _normed_input(x_ref, nw_ref):
    x = x_ref[0]
    return (_rms_lanes(x) * nw_ref[...]).astype(MXU_DTYPE)


def _causal_conv_silu(ext_ref, raw, conv_w_ref, conv_b_ref, first_block):
    ts = raw.shape[0]

    @pl.when(first_block)
    def _():
        ext_ref[0:CONV_PAD, :] = jnp.zeros((CONV_PAD, raw.shape[1]), F32)

    ext_ref[CONV_PAD:CONV_PAD + ts, :] = raw
    taps = conv_w_ref.shape[0]
    y = conv_b_ref[...]
    for j in range(taps):
        off = CONV_PAD - (taps - 1) + j
        y = y + conv_w_ref[j:j + 1, :] * ext_ref[off:off + ts, :]
    ext_ref[0:CONV_PAD, :] = ext_ref[ts:ts + CONV_PAD, :]
    return _silu(y)


def _diff_proj_kernel(x_ref, nw_ref, w_ref, cos_ref, sin_ref, qt_ref, k_ref, vt_ref, z_ref):
    u = _normed_input(x_ref, nw_ref)
    proj = jnp.dot(u, w_ref[...], preferred_element_type=F32)
    ts = proj.shape[0]
    cos = cos_ref[...]
    sin = jnp.where(lax.broadcasted_iota(jnp.int32, (1, LANES), 1) < 64, -sin_ref[...], sin_ref[...])
    ones_row = jnp.where(lax.broadcasted_iota(jnp.int32, (VT_ROWS - DIFF_DV, ts), 0) == 0, 1.0, 0.0)
    half = (lax.broadcasted_iota(jnp.int32, (1, LANES), 1) % 64) // 32

    def rope(a):
        return a * cos + jnp.concatenate([a[:, 64:], a[:, :64]], axis=1) * sin

    for h in range(DIFF_HEADS):
        sl = slice(h * LANES, (h + 1) * LANES)
        q = rope(proj[:, sl]) * (DIFF_DK ** -0.5 * LOG2E)
        for t in range(2):
            qt_ref[0, 2 * h + t, 0] = jnp.where(half == t, q, 0.0).T.astype(qt_ref.dtype)
        k_ref[0, :, sl] = rope(proj[:, 512 + h * LANES:512 + (h + 1) * LANES]).astype(k_ref.dtype)
        vt_ref[0, h, 0, 0:DIFF_DV, :] = proj[:, 1024 + h * DIFF_DV:1024 + (h + 1) * DIFF_DV].T.astype(vt_ref.dtype)
        vt_ref[0, h, 0, DIFF_DV:VT_ROWS, :] = ones_row.astype(vt_ref.dtype)
    z_ref[0] = _silu(proj[:, 1536:2048])


def _diff_flash_kernel(lam_init, qt_ref, k_ref, vt_ref, z_ref, lp_ref, nw_ref, o_ref, m_ref, acc_ref,
                       st_ref, pt_ref):
    tq = qt_ref.shape[-1]
    tk = vt_ref.shape[-1]
    i = pl.program_id(2)
    m_ref[...] = jnp.full(m_ref.shape, NEG_INF, F32)
    acc_ref[...] = jnp.zeros(acc_ref.shape, F32)

    def scores(j, slot):
        r0 = pl.multiple_of(j * tk, tk)
        for g in range(4):
            hh = g // 2
            kj = k_ref[0, pl.ds(r0, tk), hh * LANES:(hh + 1) * LANES]
            st_ref[slot, g] = jnp.dot(kj, qt_ref[0, g, 0], preferred_element_type=F32)

    def softmax(slot, masked=False):
        if masked:
            keep = (lax.broadcasted_iota(jnp.int32, (tk, tq), 0) <= lax.broadcasted_iota(jnp.int32, (tk, tq), 1))
        alphas = []
        for g in range(4):
            st = st_ref[slot, g]
            if masked:
                st = jnp.where(keep, st, NEG_INF)
            m_prev = m_ref[g]
            m_new = jnp.maximum(m_prev, jnp.max(st, axis=0, keepdims=True))
            pt_ref[slot, g] = jnp.exp2(st - jnp.concatenate([m_new] * (tk // 8), axis=0)).astype(MXU_DTYPE)
            m_ref[g] = m_new
            alphas.append(jnp.exp2(m_prev - m_new))
        return tuple(alphas)

    def pv(j, slot, alphas):
        for g in range(4):
            upd = jnp.dot(vt_ref[0, g // 2, j], pt_ref[slot, g], preferred_element_type=F32)
            acc_ref[g] = jnp.concatenate([alphas[g]] * (VT_ROWS // 8), axis=0) * acc_ref[g] + upd

    scores(0, 0)

    def pair(jj, carry):
        j = 2 * jj
        scores(j + 1, 1)
        pv(j, 0, softmax(0))
        scores(j + 2, 0)
        pv(j + 1, 1, softmax(1))
        return carry

    lax.fori_loop(0, i // 2, pair, 0)

    @pl.when(i % 2 == 0)
    def _():
        pv(i, 0, softmax(0, masked=True))

    @pl.when(i % 2 == 1)
    def _():
        scores(i, 1)
        pv(i - 1, 0, softmax(0))
        pv(i, 1, softmax(1, masked=True))

    lp = lp_ref[...]
    lam = (jnp.exp(jnp.sum(lp[0:1] * lp[1:2], axis=-1, keepdims=True))
           - jnp.exp(jnp.sum(lp[2:3] * lp[3:4], axis=-1, keepdims=True)) + lam_init)
    for hh in range(2):
        a1, a2 = acc_ref[2 * hh], acc_ref[2 * hh + 1]
        o1 = a1[0:DIFF_DV] / a1[DIFF_DV:DIFF_DV + 1]
        o2 = a2[0:DIFF_DV] / a2[DIFF_DV:DIFF_DV + 1]
        y = _rms_lanes((o1 - lam * o2).T) * nw_ref[...] * (1.0 - lam_init)
        sl = slice(hh * LANES, (hh + 1) * LANES)
        o_ref[0, :, sl] = (y * z_ref[0, :, sl]).astype(o_ref.dtype)


def _ret_kernel(x_ref, nw_ref, w_ref, cos_ref, sin_ref, dec_ref, qw_ref, kw_ref, cd_ref, msk_ref,
                o_ref, st_ref, proj_ref):
    ts = x_ref.shape[1]

    @pl.when(pl.program_id(1) == 0)
    def _():
        st_ref[...] = jnp.zeros(st_ref.shape, F32)

    proj_ref[...] = jnp.dot(_normed_input(x_ref, nw_ref), w_ref[...], preferred_element_type=F32)
    hid = (lax.broadcasted_iota(jnp.int32, (1, 256), 1) % LANES) // 32

    def chunk(c, carry):
        r0 = pl.multiple_of(c * CHUNK, CHUNK)
        rows = pl.ds(r0, CHUNK)
        cos, sin = cos_ref[rows, :], sin_ref[rows, :]
        qr = _rope(proj_ref[rows, 0:256], cos, sin)
        kr = _rope(proj_ref[rows, 256:512], cos, sin) * (RET_DK ** -0.5)
        v = proj_ref[rows, 512:1024].astype(MXU_DTYPE)
        krb = kr.astype(MXU_DTYPE)
        state = st_ref[...]
        inter = _dot(qr * qw_ref[...], state)
        st_ref[...] = state * cd_ref[...] + _dot_tn(kr * kw_ref[...], v) * msk_ref[...]
        for h in range(RET_HEADS):
            sl = slice(h * RET_DV, (h + 1) * RET_DV)
            sc = _dot_nt(jnp.where(hid == h, qr, 0.0), krb) * dec_ref[h]
            y = _rms_lanes(_dot(sc, v[:, sl]) + inter[:, sl])
            z = proj_ref[rows, 1024 + h * RET_DV:1024 + (h + 1) * RET_DV]
            o_ref[0, rows, sl] = (y * _silu(z)).astype(o_ref.dtype)
        return carry

    lax.fori_loop(0, ts // CHUNK, chunk, 0, unroll=True)


def _ssd_kernel(x_ref, nw_ref, w_ref, cw_ref, cb_ref, dtb_ref, alog_ref, dskip_ref, onw_ref,
                o_ref, st_ref, ext_ref, act_ref, dt_ref, z_ref):
    ts = x_ref.shape[1]
    first = pl.program_id(1) == 0

    @pl.when(first)
    def _():
        st_ref[...] = jnp.zeros(st_ref.shape, F32)

    proj = jnp.dot(_normed_input(x_ref, nw_ref), w_ref[...], preferred_element_type=F32)
    act_ref[...] = _causal_conv_silu(ext_ref, proj[:, 0:SSD_XBC], cw_ref, cb_ref, first)
    dt_ref[...] = _softplus(proj[:, SSD_XBC:SSD_XBC + GROUP_W] + dtb_ref[...])
    z_ref[...] = _silu(proj[:, SSD_XBC + GROUP_W:SSD_XBC + 2 * GROUP_W])
    a_neg = -jnp.exp(alog_ref[...])
    tri = _tri(CHUNK)
    tri_b = tri.astype(jnp.bfloat16)
    lane = lax.broadcasted_iota(jnp.int32, (1, LANES), 1)
    gw = GROUP_W // SSD_GROUPS

    def chunk(c, carry):
        r0 = pl.multiple_of(c * CHUNK, CHUNK)
        rows = pl.ds(r0, CHUNK)
        xs = act_ref[rows, 0:GROUP_W]
        dt = dt_ref[rows, :]
        cs = _cumsum_rows(dt * a_neg, tri_b)
        cs_last = cs[CHUNK - 1:CHUNK, :]
        xdt = xs * dt
        y_parts = []
        for g in range(SSD_GROUPS):
            bm = act_ref[rows, GROUP_W + g * SSD_STATE:GROUP_W + (g + 1) * SSD_STATE]
            cm = act_ref[rows, GROUP_W + (SSD_GROUPS + g) * SSD_STATE:GROUP_W + (SSD_GROUPS + g + 1) * SSD_STATE]
            cb = _dot_nt(cm, bm)
            gs = slice(g * gw, (g + 1) * gw)
            prev = st_ref[g]
            y_off = _dot(cm, prev) * jnp.exp(cs[:, gs])
            st_ref[g] = prev * jnp.exp(cs_last[:, gs]) + _dot_tn(bm, xdt[:, gs] * jnp.exp(cs_last[:, gs] - cs[:, gs]))
            for pr in range(gw // LANES):
                ls = slice(g * gw + pr * LANES, g * gw + (pr + 1) * LANES)
                cs_t = cs[:, ls].T
                xdt_pair = xdt[:, ls]
                y_pair = None
                for e in range(2):
                    col = cs[:, ls][:, e * SSD_HEAD_DIM:e * SSD_HEAD_DIM + 1]
                    row = cs_t[e * SSD_HEAD_DIM:e * SSD_HEAD_DIM + 1, :]
                    lmat = jnp.exp(jnp.where(tri, col - row, -jnp.inf))
                    half = jnp.where((lane // SSD_HEAD_DIM) == e, xdt_pair, 0.0)
                    term = _dot(cb * lmat, half)
                    y_pair = term if y_pair is None else y_pair + term
                y_parts.append(y_pair + y_off[:, pr * LANES:(pr + 1) * LANES])
        y = jnp.concatenate(y_parts, axis=-1) + xs * dskip_ref[...]
        o_ref[0, rows, :] = (_rms_lanes(y * z_ref[rows, :]) * onw_ref[...]).astype(o_ref.dtype)
        return carry

    lax.fori_loop(0, ts // CHUNK, chunk, 0, unroll=True)


def _mlstm_kernel(x_ref, nw_ref, w_ref, cw_ref, cb_ref, ib_ref, fb_ref, onw_ref,
                  o_ref, ct_ref, n_ref, m_ref, ext_ref, act_ref, proj_ref):
    ts = x_ref.shape[1]
    first = pl.program_id(1) == 0
    nh, dh = MLSTM_HEADS, MLSTM_DH

    @pl.when(first)
    def _():
        ct_ref[...] = jnp.zeros(ct_ref.shape, F32)
        n_ref[...] = jnp.zeros(n_ref.shape, F32)
        m_ref[...] = jnp.zeros(m_ref.shape, F32)

    proj_ref[...] = jnp.dot(_normed_input(x_ref, nw_ref), w_ref[...], preferred_element_type=F32)
    act_ref[...] = _causal_conv_silu(ext_ref, proj_ref[:, 0:2 * GROUP_W], cw_ref, cb_ref, first)
    tri = _tri(CHUNK)
    tri_b = tri.astype(jnp.bfloat16)
    c_v, c_o, c_i, c_f, c_z = (2 * GROUP_W + k * GROUP_W for k in range(5))

    def chunk(c, carry):
        r0 = pl.multiple_of(c * CHUNK, CHUNK)
        rows = pl.ds(r0, CHUNK)
        ig = proj_ref[rows, c_i:c_i + GROUP_W] + ib_ref[...]
        fg = -_softplus(-(proj_ref[rows, c_f:c_f + GROUP_W] + fb_ref[...]))
        bcs = _cumsum_rows(fg, tri_b)
        for h in range(nh):
            sl = slice(h * dh, (h + 1) * dh)
            b_col = bcs[:, sl]
            b_row = b_col.T
            i_col = ig[:, sl]
            d_log = jnp.where(tri, b_col - b_row + i_col.T, -jnp.inf)
            m_prev = m_ref[h:h + 1, :]
            inter_log = b_col + m_prev
            m_row = jnp.maximum(jnp.max(d_log, axis=-1, keepdims=True), inter_log)
            qh = act_ref[rows, sl]
            kh = act_ref[rows, GROUP_W + h * dh:GROUP_W + (h + 1) * dh] * (dh ** -0.5)
            vh = proj_ref[rows, c_v + h * dh:c_v + (h + 1) * dh].astype(MXU_DTYPE)
            s = _dot_nt(qh, kh) * jnp.exp(d_log - m_row)
            inter_w = jnp.exp(inter_log - m_row)
            ct = ct_ref[h]
            n_row = n_ref[h:h + 1, :]
            num = _dot(s, vh) + inter_w * _dot(qh, ct)
            qn = jnp.sum(s, axis=-1, keepdims=True) + inter_w * jnp.sum(qh * n_row, axis=-1, keepdims=True)
            hv = num / jnp.maximum(jnp.abs(qn), jnp.exp(-m_row))
            b_last = b_col[CHUNK - 1:CHUNK, :]
            w_log = b_last - b_col + i_col
            m_new = jnp.maximum(b_last + m_prev, jnp.max(w_log, axis=0, keepdims=True))
            kw = kh * jnp.exp(w_log - m_new)
            decay = jnp.exp(b_last + m_prev - m_new)
            ct_ref[h] = decay * ct + _dot_tn(kw, vh)
            n_ref[h:h + 1, :] = decay * n_row + jnp.sum(kw, axis=0, keepdims=True)
            m_ref[h:h + 1, :] = m_new
            hv = _sigmoid(proj_ref[rows, c_o + h * dh:c_o + (h + 1) * dh]) * hv
            hv = _rms_lanes(hv - jnp.mean(hv, axis=-1, keepdims=True)) * onw_ref[:, sl]
            o_ref[0, rows, sl] = (hv * _silu(proj_ref[rows, c_z + h * dh:c_z + (h + 1) * dh])).astype(o_ref.dtype)
        return carry

    lax.fori_loop(0, ts // CHUNK, chunk, 0, unroll=True)


def _out_kernel(final_norm, h_ref, a_ref, b_ref, c_ref, d_ref, w_ref, fw_ref, o_ref):
    acc = h_ref[0]
    for k, m_ref in enumerate((a_ref, b_ref, c_ref, d_ref)):
        acc = acc + jnp.dot(m_ref[0], w_ref[k * GROUP_W:(k + 1) * GROUP_W, :], preferred_element_type=F32)
    if final_norm:
        acc = _rms_lanes(acc) * fw_ref[...]
    o_ref[0] = acc


def _params(*sem):
    return pltpu.CompilerParams(dimension_semantics=sem, vmem_limit_bytes=VMEM_LIMIT)


def _const_spec(shape):
    nd = len(shape)
    return pl.BlockSpec(shape, lambda *_: (0,) * nd)


def _rope_perm(groups, per_block):
    idx = []
    for blk in range(groups // per_block):
        for e in range(2):
            for g in range(per_block):
                base = (blk * per_block + g) * 64 + e * 32
                idx.extend(range(base, base + 32))
    return np.asarray(idx, np.int32)


def _retention_tables():
    nh, L = RET_HEADS, CHUNK
    log_g = jnp.log(1.0 - jnp.exp2(-5.0 - jnp.arange(nh, dtype=F32)))
    pos = jnp.arange(L, dtype=F32)
    rel = pos[:, None] - pos[None, :]
    decay = jnp.where(rel >= 0, jnp.exp(log_g[:, None, None] * jnp.maximum(rel, 0.0)), 0.0)
    head_of_col = (np.arange(256) % LANES) // 32
    q_w = jnp.exp(log_g[None, :] * (pos + 1.0)[:, None])[:, head_of_col]
    k_w = jnp.exp(log_g[None, :] * (L - 1.0 - pos)[:, None])[:, head_of_col]
    same_head = jnp.asarray(head_of_col[:, None] == (np.arange(GROUP_W) // RET_DV)[None, :], F32)
    chunk_decay = jnp.exp(log_g * L)[head_of_col][:, None] * same_head
    return decay, q_w, k_w, chunk_decay, same_head


def kernel(x, norm_w, w_in, w_out, diff_lambda, diff_norm_w, ssd_conv_w, ssd_conv_b, ssd_dt_bias, ssd_a_log,
           ssd_d, ssd_norm_w, mlstm_conv_w, mlstm_conv_b, mlstm_gate_b, mlstm_norm_w, final_norm_w):
    bsz, seq, dm = x.shape
    depth = w_in.shape[0]
    ts = min(TOKEN_BLOCK, seq)
    tq = ts
    nblk = seq // ts
    bf = MXU_DTYPE

    inv = ROPE_THETA ** (-jnp.arange(0, RET_DK, 2, dtype=F32) / RET_DK)
    ang = jnp.arange(seq, dtype=F32)[:, None] * inv[None, :]
    cos4, sin4 = jnp.tile(jnp.cos(ang), (1, 4)), jnp.tile(jnp.sin(ang), (1, 4))
    ret_tabs = _retention_tables()
    perm_ret, perm_diff = _rope_perm(RET_HEADS, 4), _rope_perm(2 * DIFF_HEADS, 2)

    x_spec = pl.BlockSpec((1, ts, dm), lambda b, s: (b, s, 0))
    row_spec = pl.BlockSpec((ts, LANES), lambda b, s: (s, 0))
    grp_spec = pl.BlockSpec((1, ts, GROUP_W), lambda b, s: (b, s, 0))
    grp_shape = jax.ShapeDtypeStruct((bsz, seq, GROUP_W), bf)

    h = x
    for l in range(depth):
        w = w_in[l]
        nw = norm_w[l].reshape(1, dm)
        rep = lambda a, n: jnp.repeat(a, n, axis=-1)

        w_diff = jnp.concatenate([w[:, DIFF_OFF:DIFF_OFF + 512][:, perm_diff],
                                  w[:, DIFF_OFF + 512:DIFF_OFF + 1024][:, perm_diff],
                                  w[:, DIFF_OFF + 1024:DIFF_OFF + 2048]], axis=1).astype(bf)
        nkb = seq // tq
        dqt, dk, dvt, dz = pl.pallas_call(
            _diff_proj_kernel,
            grid=(bsz, nblk),
            in_specs=[x_spec, _const_spec((1, dm)), _const_spec(w_diff.shape), row_spec, row_spec],
            out_specs=[pl.BlockSpec((1, 2 * DIFF_HEADS, 1, LANES, ts), lambda b, s: (b, 0, s, 0, 0)),
                       grp_spec,
                       pl.BlockSpec((1, DIFF_HEADS, 1, VT_ROWS, ts), lambda b, s: (b, 0, s, 0, 0)),
                       grp_spec],
            out_shape=[jax.ShapeDtypeStruct((bsz, 2 * DIFF_HEADS, nkb, LANES, ts), bf), grp_shape,
                       jax.ShapeDtypeStruct((bsz, DIFF_HEADS, nkb, VT_ROWS, ts), bf),
                       jax.ShapeDtypeStruct((bsz, seq, GROUP_W), F32)],
            compiler_params=_params("parallel", "parallel"),
            name="diff_proj",
        )(h, nw, w_diff, cos4, sin4)

        lam_init = 0.8 - 0.6 * math.exp(-0.3 * l)
        pair_q = pl.BlockSpec((1, tq, 256), lambda b, p, i: (b, i, p))
        diff_out = pl.pallas_call(
            functools.partial(_diff_flash_kernel, lam_init),
            grid=(bsz, 2, seq // tq),
            in_specs=[pl.BlockSpec((1, 4, 1, LANES, tq), lambda b, p, i: (b, p, i, 0, 0)),
                      pl.BlockSpec((1, seq, 256), lambda b, p, i: (b, 0, p)),
                      pl.BlockSpec((1, 2, nkb, VT_ROWS, tq), lambda b, p, i: (b, p, 0, 0, 0)),
                      pair_q, _const_spec((4, DIFF_DK)), _const_spec((1, DIFF_DV))],
            out_specs=pair_q,
            out_shape=grp_shape,
            scratch_shapes=[pltpu.VMEM((4, 8, tq), F32),
                            pltpu.VMEM((4, VT_ROWS, tq), F32), pltpu.VMEM((2, 4, tq, tq), F32),
                            pltpu.VMEM((2, 4, tq, tq), bf)],
            compiler_params=_params("parallel", "parallel", "arbitrary"),
            name="diff_flash",
        )(dqt, dk, dvt, dz, diff_lambda[l].astype(F32), diff_norm_w[l].astype(F32).reshape(1, DIFF_DV))

        w_ret = jnp.concatenate([w[:, RET_OFF:RET_OFF + 256][:, perm_ret],
                                 w[:, RET_OFF + 256:RET_OFF + 512][:, perm_ret],
                                 w[:, RET_OFF + 512:RET_OFF + 1536]], axis=1).astype(bf)
        ret_out = pl.pallas_call(
            _ret_kernel,
            grid=(bsz, nblk),
            in_specs=[x_spec, _const_spec((1, dm)), _const_spec(w_ret.shape), row_spec, row_spec]
                     + [_const_spec(t.shape) for t in ret_tabs],
            out_specs=grp_spec,
            out_shape=grp_shape,
            scratch_shapes=[pltpu.VMEM((256, GROUP_W), F32), pltpu.VMEM((ts, w_ret.shape[1]), F32)],
            compiler_params=_params("parallel", "arbitrary"),
            name="retention",
        )(h, nw, w_ret, cos4, sin4, *ret_tabs)

        w_ssd = jnp.concatenate([w[:, SSD_OFF:SSD_OFF + SSD_XBC],
                                 rep(w[:, SSD_OFF + SSD_XBC:SSD_OFF + SSD_XBC + SSD_HEADS], SSD_HEAD_DIM),
                                 w[:, SSD_OFF + SSD_XBC + SSD_HEADS:MLSTM_OFF]], axis=1).astype(bf)
        head_row = lambda a: rep(a.astype(F32), SSD_HEAD_DIM).reshape(1, GROUP_W)
        ssd_out = pl.pallas_call(
            _ssd_kernel,
            grid=(bsz, nblk),
            in_specs=[x_spec, _const_spec((1, dm)), _const_spec(w_ssd.shape), _const_spec((SSD_CONV, SSD_XBC)),
                      _const_spec((1, SSD_XBC))] + [_const_spec((1, GROUP_W))] * 4,
            out_specs=grp_spec,
            out_shape=grp_shape,
            scratch_shapes=[pltpu.VMEM((SSD_GROUPS, SSD_STATE, GROUP_W // SSD_GROUPS), F32),
                            pltpu.VMEM((ts + CONV_PAD, SSD_XBC), F32), pltpu.VMEM((ts, SSD_XBC), F32),
                            pltpu.VMEM((ts, GROUP_W), F32), pltpu.VMEM((ts, GROUP_W), F32)],
            compiler_params=_params("parallel", "arbitrary"),
            name="ssd",
        )(h, nw, w_ssd, ssd_conv_w[l].astype(F32), ssd_conv_b[l].astype(F32).reshape(1, SSD_XBC),
          head_row(ssd_dt_bias[l]), head_row(ssd_a_log[l]), head_row(ssd_d[l]),
          ssd_norm_w[l].astype(F32).reshape(1, GROUP_W))

        g0 = MLSTM_OFF + 4 * GROUP_W
        w_ml = jnp.concatenate([w[:, MLSTM_OFF:g0],
                                rep(w[:, g0:g0 + MLSTM_HEADS], MLSTM_DH),
                                rep(w[:, g0 + MLSTM_HEADS:g0 + 2 * MLSTM_HEADS], MLSTM_DH),
                                w[:, g0 + 2 * MLSTM_HEADS:g0 + 2 * MLSTM_HEADS + GROUP_W]], axis=1).astype(bf)
        gate_b = mlstm_gate_b[l].astype(F32)
        ml_out = pl.pallas_call(
            _mlstm_kernel,
            grid=(bsz, nblk),
            in_specs=[x_spec, _const_spec((1, dm)), _const_spec(w_ml.shape),
                      _const_spec((MLSTM_CONV, 2 * GROUP_W)), _const_spec((1, 2 * GROUP_W))]
                     + [_const_spec((1, GROUP_W))] * 3,
            out_specs=grp_spec,
            out_shape=grp_shape,
            scratch_shapes=[pltpu.VMEM((MLSTM_HEADS, MLSTM_DH, MLSTM_DH), F32), pltpu.VMEM((8, MLSTM_DH), F32),
                            pltpu.VMEM((8, LANES), F32), pltpu.VMEM((ts + CONV_PAD, 2 * GROUP_W), F32),
                            pltpu.VMEM((ts, 2 * GROUP_W), F32), pltpu.VMEM((ts, w_ml.shape[1]), F32)],
            compiler_params=_params("parallel", "arbitrary"),
            name="mlstm",
        )(h, nw, w_ml, mlstm_conv_w[l].astype(F32), mlstm_conv_b[l].astype(F32).reshape(1, 2 * GROUP_W),
          rep(gate_b[:MLSTM_HEADS], MLSTM_DH).reshape(1, GROUP_W),
          rep(gate_b[MLSTM_HEADS:], MLSTM_DH).reshape(1, GROUP_W),
          mlstm_norm_w[l].astype(F32).reshape(1, GROUP_W))

        last = l == depth - 1
        h = pl.pallas_call(
            functools.partial(_out_kernel, last),
            grid=(bsz, nblk),
            in_specs=[x_spec] + [grp_spec] * 4 + [_const_spec((4 * GROUP_W, dm)), _const_spec((1, dm))],
            out_specs=x_spec,
            out_shape=jax.ShapeDtypeStruct((bsz, seq, dm), F32),
            compiler_params=_params("parallel", "parallel"),
            name="out_proj",
        )(h, ret_out, diff_out, ssd_out, ml_out, w_out[l].astype(bf), final_norm_w.astype(F32).reshape(1, dm))
    return h
```

```python
import functools
import math

import numpy as np
import jax
import jax.numpy as jnp
from jax import lax
from jax.experimental import pallas as pl
from jax.experimental.pallas import tpu as pltpu

F32 = jnp.float32
MXU_DTYPE = jnp.bfloat16

D_MODEL = 1024
GROUP_W = 512
CHUNK = 128
ROPE_THETA = 10000.0
EPS = 1e-6
NEG_INF = -1e30
LOG2E = math.log2(math.e)

RET_HEADS, RET_DK, RET_DV = 4, 64, 128
DIFF_HEADS, DIFF_DK, DIFF_DV = 4, 64, 128
SSD_HEADS, SSD_HEAD_DIM, SSD_GROUPS, SSD_STATE, SSD_CONV = 8, 64, 2, 128, 4
SSD_XBC = GROUP_W + 2 * SSD_GROUPS * SSD_STATE
MLSTM_HEADS, MLSTM_DH, MLSTM_CONV = 4, 128, 4

RET_OFF = 0
DIFF_OFF = RET_OFF + 2 * RET_HEADS * RET_DK + 2 * GROUP_W
SSD_OFF = DIFF_OFF + 4 * GROUP_W
MLSTM_OFF = SSD_OFF + SSD_XBC + SSD_HEADS + GROUP_W

LANES = 128
CONV_PAD = 8
VT_ROWS = DIFF_DV + 16
VMEM_LIMIT = 56 * 1024 * 1024

TOKEN_BLOCK = 512


def _dot(a, b):
    return jnp.dot(a.astype(MXU_DTYPE), b.astype(MXU_DTYPE), preferred_element_type=F32)


def _dot_nt(a, b):
    return lax.dot_general(a.astype(MXU_DTYPE), b.astype(MXU_DTYPE), (((1,), (1,)), ((), ())),
                           preferred_element_type=F32)


def _dot_tn(a, b):
    return lax.dot_general(a.astype(MXU_DTYPE), b.astype(MXU_DTYPE), (((0,), (0,)), ((), ())),
                           preferred_element_type=F32)


def _sigmoid(x):
    return 1.0 / (1.0 + jnp.exp(-x))


def _silu(x):
    return x * _sigmoid(x)


def _softplus(x):
    return jnp.maximum(x, 0.0) + jnp.log1p(jnp.exp(-jnp.abs(x)))


def _rms_lanes(x):
    return x * lax.rsqrt(jnp.mean(x * x, axis=-1, keepdims=True) + EPS)


def _rope(a, cos, sin):
    x1, x2 = a[:, :LANES], a[:, LANES:]
    return jnp.concatenate([x1 * cos - x2 * sin, x2 * cos + x1 * sin], axis=-1)


def _cumsum_rows(x, tri):
    hi = x.astype(jnp.bfloat16)
    r1 = x - hi.astype(F32)
    mid = r1.astype(jnp.bfloat16)
    lo = (r1 - mid.astype(F32)).astype(jnp.bfloat16)
    mm = lambda t: jnp.dot(tri, t, preferred_element_type=F32)
    return mm(hi) + mm(mid) + mm(lo)


def _tri(n):
    row = lax.broadcasted_iota(jnp.int32, (n, n), 0)
    col = lax.broadcasted_iota(jnp.int32, (n, n), 1)
    return col <= row


def _norm_kernel(x_ref, nw_ref, u_ref):
    u_ref[0] = (_rms_lanes(x_ref[0]) * nw_ref[...]).astype(u_ref.dtype)


def _causal_conv_silu(ext_ref, raw, conv_w_ref, conv_b_ref, first_block):
    ts = raw.shape[0]

    @pl.when(first_block)
    def _():
        ext_ref[0:CONV_PAD, :] = jnp.zeros((CONV_PAD, raw.shape[1]), F32)

    ext_ref[CONV_PAD:CONV_PAD + ts, :] = raw
    taps = conv_w_ref.shape[0]
    y = conv_b_ref[...]
    for j in range(taps):
        off = CONV_PAD - (taps - 1) + j
        y = y + conv_w_ref[j:j + 1, :] * ext_ref[off:off + ts, :]
    ext_ref[0:CONV_PAD, :] = ext_ref[ts:ts + CONV_PAD, :]
    return _silu(y)


def _diff_proj_kernel(u_ref, w_ref, cos_ref, sin_ref, qt_ref, k_ref, vt_ref, z_ref):
    proj = jnp.dot(u_ref[0], w_ref[...], preferred_element_type=F32)
    ts = proj.shape[0]
    cos = cos_ref[...]
    sin = jnp.where(lax.broadcasted_iota(jnp.int32, (1, LANES), 1) < 64, -sin_ref[...], sin_ref[...])
    ones_row = jnp.where(lax.broadcasted_iota(jnp.int32, (VT_ROWS - DIFF_DV, ts), 0) == 0, 1.0, 0.0)
    half = (lax.broadcasted_iota(jnp.int32, (1, LANES), 1) % 64) // 32

    def rope(a):
        return a * cos + jnp.concatenate([a[:, 64:], a[:, :64]], axis=1) * sin

    for h in range(DIFF_HEADS):
        sl = slice(h * LANES, (h + 1) * LANES)
        q = rope(proj[:, sl]) * (DIFF_DK ** -0.5 * LOG2E)
        for t in range(2):
            qt_ref[0, 2 * h + t, 0] = jnp.where(half == t, q, 0.0).T.astype(qt_ref.dtype)
        k_ref[0, :, sl] = rope(proj[:, 512 + h * LANES:512 + (h + 1) * LANES]).astype(k_ref.dtype)
        vt_ref[0, h, 0, 0:DIFF_DV, :] = proj[:, 1024 + h * DIFF_DV:1024 + (h + 1) * DIFF_DV].T.astype(vt_ref.dtype)
        vt_ref[0, h, 0, DIFF_DV:VT_ROWS, :] = ones_row.astype(vt_ref.dtype)
    z_ref[0] = _silu(proj[:, 1536:2048])


def _diff_flash_kernel(lam_init, qt_ref, k_ref, vt_ref, z_ref, lp_ref, nw_ref, o_ref, m_ref, acc_ref,
                       st_ref, pt_ref):
    tq = qt_ref.shape[-1]
    tk = vt_ref.shape[-1]
    i = pl.program_id(2)
    m_ref[...] = jnp.full(m_ref.shape, NEG_INF, F32)
    acc_ref[...] = jnp.zeros(acc_ref.shape, F32)

    def scores(j, g, slot):
        r0 = pl.multiple_of(j * tk, tk)
        kj = k_ref[0, pl.ds(r0, tk), (g // 2) * LANES:(g // 2 + 1) * LANES]
        st_ref[slot] = jnp.dot(kj, qt_ref[0, g, 0], preferred_element_type=F32)

    def softmax_pv(j, g, slot, masked):
        st = st_ref[slot]
        if masked:
            keep = (lax.broadcasted_iota(jnp.int32, (tk, tq), 0) <= lax.broadcasted_iota(jnp.int32, (tk, tq), 1))
            st = jnp.where(keep, st, NEG_INF)
        m_prev = m_ref[g]
        m_new = jnp.maximum(m_prev, jnp.max(st, axis=0, keepdims=True))
        pt_ref[g] = jnp.exp2(st - jnp.concatenate([m_new] * (tk // 8), axis=0)).astype(MXU_DTYPE)
        m_ref[g] = m_new
        alpha = jnp.exp2(m_prev - m_new)
        upd = jnp.dot(vt_ref[0, g // 2, j], pt_ref[g], preferred_element_type=F32)
        acc_ref[g] = jnp.concatenate([alpha] * (VT_ROWS // 8), axis=0) * acc_ref[g] + upd

    def block(j, masked):
        for g in range(4):
            if g < 3:
                scores(j, g + 1, (g + 1) % 2)
            elif not masked:
                scores(j + 1, 0, 0)
            softmax_pv(j, g, g % 2, masked)

    scores(0, 0, 0)

    def body(j, carry):
        block(j, False)
        return carry

    lax.fori_loop(0, i, body, 0)
    block(i, True)

    lp = lp_ref[...]
    lam = (jnp.exp(jnp.sum(lp[0:1] * lp[1:2], axis=-1, keepdims=True))
           - jnp.exp(jnp.sum(lp[2:3] * lp[3:4], axis=-1, keepdims=True)) + lam_init)
    for hh in range(2):
        a1, a2 = acc_ref[2 * hh], acc_ref[2 * hh + 1]
        o1 = a1[0:DIFF_DV] / a1[DIFF_DV:DIFF_DV + 1]
        o2 = a2[0:DIFF_DV] / a2[DIFF_DV:DIFF_DV + 1]
        y = _rms_lanes((o1 - lam * o2).T) * nw_ref[...] * (1.0 - lam_init)
        sl = slice(hh * LANES, (hh + 1) * LANES)
        o_ref[0, :, sl] = (y * z_ref[0, :, sl]).astype(o_ref.dtype)


def _ret_kernel(u_ref, w_ref, cos_ref, sin_ref, dec_ref, qw_ref, kw_ref, cd_ref, msk_ref,
                o_ref, st_ref, proj_ref):
    ts = u_ref.shape[1]

    @pl.when(pl.program_id(1) == 0)
    def _():
        st_ref[...] = jnp.zeros(st_ref.shape, F32)

    proj_ref[...] = jnp.dot(u_ref[0], w_ref[...], preferred_element_type=F32)
    hid = (lax.broadcasted_iota(jnp.int32, (1, 256), 1) % LANES) // 32

    def chunk(c, carry):
        r0 = pl.multiple_of(c * CHUNK, CHUNK)
        rows = pl.ds(r0, CHUNK)
        cos, sin = cos_ref[rows, :], sin_ref[rows, :]
        qr = _rope(proj_ref[rows, 0:256], cos, sin)
        kr = _rope(proj_ref[rows, 256:512], cos, sin) * (RET_DK ** -0.5)
        v = proj_ref[rows, 512:1024].astype(MXU_DTYPE)
        krb = kr.astype(MXU_DTYPE)
        state = st_ref[...]
        inter = _dot(qr * qw_ref[...], state)
        st_ref[...] = state * cd_ref[...] + _dot_tn(kr * kw_ref[...], v) * msk_ref[...]
        for h in range(RET_HEADS):
            sl = slice(h * RET_DV, (h + 1) * RET_DV)
            sc = _dot_nt(jnp.where(hid == h, qr, 0.0), krb) * dec_ref[h]
            y = _rms_lanes(_dot(sc, v[:, sl]) + inter[:, sl])
            z = proj_ref[rows, 1024 + h * RET_DV:1024 + (h + 1) * RET_DV]
            o_ref[0, rows, sl] = (y * _silu(z)).astype(o_ref.dtype)
        return carry

    lax.fori_loop(0, ts // CHUNK, chunk, 0, unroll=True)


def _ssd_kernel(u_ref, w_ref, cw_ref, cb_ref, dtb_ref, alog_ref, dskip_ref, onw_ref,
                o_ref, st_ref, ext_ref, act_ref, dt_ref, z_ref):
    ts = u_ref.shape[1]
    first = pl.program_id(1) == 0

    @pl.when(first)
    def _():
        st_ref[...] = jnp.zeros(st_ref.shape, F32)

    proj = jnp.dot(u_ref[0], w_ref[...], preferred_element_type=F32)
    act_ref[...] = _causal_conv_silu(ext_ref, proj[:, 0:SSD_XBC], cw_ref, cb_ref, first)
    dt_ref[...] = _softplus(proj[:, SSD_XBC:SSD_XBC + GROUP_W] + dtb_ref[...])
    z_ref[...] = _silu(proj[:, SSD_XBC + GROUP_W:SSD_XBC + 2 * GROUP_W])
    a_neg = -jnp.exp(alog_ref[...])
    tri = _tri(CHUNK)
    tri_b = tri.astype(jnp.bfloat16)
    lane = lax.broadcasted_iota(jnp.int32, (1, LANES), 1)
    gw = GROUP_W // SSD_GROUPS

    def chunk(c, carry):
        r0 = pl.multiple_of(c * CHUNK, CHUNK)
        rows = pl.ds(r0, CHUNK)
        xs = act_ref[rows, 0:GROUP_W]
        dt = dt_ref[rows, :]
        cs = _cumsum_rows(dt * a_neg, tri_b)
        cs_last = cs[CHUNK - 1:CHUNK, :]
        xdt = xs * dt
        y_parts = []
        for g in range(SSD_GROUPS):
            bm = act_ref[rows, GROUP_W + g * SSD_STATE:GROUP_W + (g + 1) * SSD_STATE]
            cm = act_ref[rows, GROUP_W + (SSD_GROUPS + g) * SSD_STATE:GROUP_W + (SSD_GROUPS + g + 1) * SSD_STATE]
            cb = _dot_nt(cm, bm)
            gs = slice(g * gw, (g + 1) * gw)
            prev = st_ref[g]
            y_off = _dot(cm, prev) * jnp.exp(cs[:, gs])
            st_ref[g] = prev * jnp.exp(cs_last[:, gs]) + _dot_tn(bm, xdt[:, gs] * jnp.exp(cs_last[:, gs] - cs[:, gs]))
            for pr in range(gw // LANES):
                ls = slice(g * gw + pr * LANES, g * gw + (pr + 1) * LANES)
                cs_t = cs[:, ls].T
                xdt_pair = xdt[:, ls]
                y_pair = None
                for e in range(2):
                    col = cs[:, ls][:, e * SSD_HEAD_DIM:e * SSD_HEAD_DIM + 1]
                    row = cs_t[e * SSD_HEAD_DIM:e * SSD_HEAD_DIM + 1, :]
                    lmat = jnp.exp(jnp.where(tri, col - row, -jnp.inf))
                    half = jnp.where((lane // SSD_HEAD_DIM) == e, xdt_pair, 0.0)
                    term = _dot(cb * lmat, half)
                    y_pair = term if y_pair is None else y_pair + term
                y_parts.append(y_pair + y_off[:, pr * LANES:(pr + 1) * LANES])
        y = jnp.concatenate(y_parts, axis=-1) + xs * dskip_ref[...]
        o_ref[0, rows, :] = (_rms_lanes(y * z_ref[rows, :]) * onw_ref[...]).astype(o_ref.dtype)
        return carry

    lax.fori_loop(0, ts // CHUNK, chunk, 0, unroll=True)


def _mlstm_kernel(u_ref, w_ref, cw_ref, cb_ref, ib_ref, fb_ref, onw_ref,
                  o_ref, ct_ref, n_ref, m_ref, ext_ref, act_ref, proj_ref):
    ts = u_ref.shape[1]
    first = pl.program_id(1) == 0
    nh, dh = MLSTM_HEADS, MLSTM_DH

    @pl.when(first)
    def _():
        ct_ref[...] = jnp.zeros(ct_ref.shape, F32)
        n_ref[...] = jnp.zeros(n_ref.shape, F32)
        m_ref[...] = jnp.zeros(m_ref.shape, F32)

    proj_ref[...] = jnp.dot(u_ref[0], w_ref[...], preferred_element_type=F32)
    act_ref[...] = _causal_conv_silu(ext_ref, proj_ref[:, 0:2 * GROUP_W], cw_ref, cb_ref, first)
    tri = _tri(CHUNK)
    tri_b = tri.astype(jnp.bfloat16)
    c_v, c_o, c_i, c_f, c_z = (2 * GROUP_W + k * GROUP_W for k in range(5))

    def chunk(c, carry):
        r0 = pl.multiple_of(c * CHUNK, CHUNK)
        rows = pl.ds(r0, CHUNK)
        ig = proj_ref[rows, c_i:c_i + GROUP_W] + ib_ref[...]
        fg = -_softplus(-(proj_ref[rows, c_f:c_f + GROUP_W] + fb_ref[...]))
        bcs = _cumsum_rows(fg, tri_b)
        for h in range(nh):
            sl = slice(h * dh, (h + 1) * dh)
            b_col = bcs[:, sl]
            b_row = b_col.T
            i_col = ig[:, sl]
            d_log = jnp.where(tri, b_col - b_row + i_col.T, -jnp.inf)
            m_prev = m_ref[h:h + 1, :]
            inter_log = b_col + m_prev
            m_row = jnp.maximum(jnp.max(d_log, axis=-1, keepdims=True), inter_log)
            qh = act_ref[rows, sl]
            kh = act_ref[rows, GROUP_W + h * dh:GROUP_W + (h + 1) * dh] * (dh ** -0.5)
            vh = proj_ref[rows, c_v + h * dh:c_v + (h + 1) * dh].astype(MXU_DTYPE)
            s = _dot_nt(qh, kh) * jnp.exp(d_log - m_row)
            inter_w = jnp.exp(inter_log - m_row)
            ct = ct_ref[h]
            n_row = n_ref[h:h + 1, :]
            num = _dot(s, vh) + inter_w * _dot(qh, ct)
            qn = jnp.sum(s, axis=-1, keepdims=True) + inter_w * jnp.sum(qh * n_row, axis=-1, keepdims=True)
            hv = num / jnp.maximum(jnp.abs(qn), jnp.exp(-m_row))
            b_last = b_col[CHUNK - 1:CHUNK, :]
            w_log = b_last - b_col + i_col
            m_new = jnp.maximum(b_last + m_prev, jnp.max(w_log, axis=0, keepdims=True))
            kw = kh * jnp.exp(w_log - m_new)
            decay = jnp.exp(b_last + m_prev - m_new)
            ct_ref[h] = decay * ct + _dot_tn(kw, vh)
            n_ref[h:h + 1, :] = decay * n_row + jnp.sum(kw, axis=0, keepdims=True)
            m_ref[h:h + 1, :] = m_new
            hv = _sigmoid(proj_ref[rows, c_o + h * dh:c_o + (h + 1) * dh]) * hv
            hv = _rms_lanes(hv - jnp.mean(hv, axis=-1, keepdims=True)) * onw_ref[:, sl]
            o_ref[0, rows, sl] = (hv * _silu(proj_ref[rows, c_z + h * dh:c_z + (h + 1) * dh])).astype(o_ref.dtype)
        return carry

    lax.fori_loop(0, ts // CHUNK, chunk, 0, unroll=True)


def _out_kernel(last, h_ref, a_ref, b_ref, c_ref, d_ref, w_ref, nw_ref, *o_refs):
    acc = h_ref[0]
    for k, m_ref in enumerate((a_ref, b_ref, c_ref, d_ref)):
        acc = acc + jnp.dot(m_ref[0], w_ref[k * GROUP_W:(k + 1) * GROUP_W, :], preferred_element_type=F32)
    normed = _rms_lanes(acc) * nw_ref[...]
    if last:
        o_refs[0][0] = normed
    else:
        o_refs[0][0] = acc
        o_refs[1][0] = normed.astype(o_refs[1].dtype)


def _params(*sem):
    return pltpu.CompilerParams(dimension_semantics=sem, vmem_limit_bytes=VMEM_LIMIT)


def _const_spec(shape):
    nd = len(shape)
    return pl.BlockSpec(shape, lambda *_: (0,) * nd)


def _rope_perm(groups, per_block):
    idx = []
    for blk in range(groups // per_block):
        for e in range(2):
            for g in range(per_block):
                base = (blk * per_block + g) * 64 + e * 32
                idx.extend(range(base, base + 32))
    return np.asarray(idx, np.int32)


def _retention_tables():
    nh, L = RET_HEADS, CHUNK
    log_g = jnp.log(1.0 - jnp.exp2(-5.0 - jnp.arange(nh, dtype=F32)))
    pos = jnp.arange(L, dtype=F32)
    rel = pos[:, None] - pos[None, :]
    decay = jnp.where(rel >= 0, jnp.exp(log_g[:, None, None] * jnp.maximum(rel, 0.0)), 0.0)
    head_of_col = (np.arange(256) % LANES) // 32
    q_w = jnp.exp(log_g[None, :] * (pos + 1.0)[:, None])[:, head_of_col]
    k_w = jnp.exp(log_g[None, :] * (L - 1.0 - pos)[:, None])[:, head_of_col]
    same_head = jnp.asarray(head_of_col[:, None] == (np.arange(GROUP_W) // RET_DV)[None, :], F32)
    chunk_decay = jnp.exp(log_g * L)[head_of_col][:, None] * same_head
    return decay, q_w, k_w, chunk_decay, same_head


def kernel(x, norm_w, w_in, w_out, diff_lambda, diff_norm_w, ssd_conv_w, ssd_conv_b, ssd_dt_bias, ssd_a_log,
           ssd_d, ssd_norm_w, mlstm_conv_w, mlstm_conv_b, mlstm_gate_b, mlstm_norm_w, final_norm_w):
    bsz, seq, dm = x.shape
    depth = w_in.shape[0]
    ts = min(TOKEN_BLOCK, seq)
    tq = ts
    nblk = seq // ts
    bf = MXU_DTYPE

    inv = ROPE_THETA ** (-jnp.arange(0, RET_DK, 2, dtype=F32) / RET_DK)
    ang = jnp.arange(seq, dtype=F32)[:, None] * inv[None, :]
    cos4, sin4 = jnp.tile(jnp.cos(ang), (1, 4)), jnp.tile(jnp.sin(ang), (1, 4))
    ret_tabs = _retention_tables()
    perm_ret, perm_diff = _rope_perm(RET_HEADS, 4), _rope_perm(2 * DIFF_HEADS, 2)

    x_spec = pl.BlockSpec((1, ts, dm), lambda b, s: (b, s, 0))
    row_spec = pl.BlockSpec((ts, LANES), lambda b, s: (s, 0))
    grp_spec = pl.BlockSpec((1, ts, GROUP_W), lambda b, s: (b, s, 0))
    grp_shape = jax.ShapeDtypeStruct((bsz, seq, GROUP_W), bf)
    u_shape = jax.ShapeDtypeStruct((bsz, seq, dm), bf)

    def layer_spec(shape, l):
        return pl.BlockSpec((None,) + tuple(shape), lambda *_: (l,) + (0,) * len(shape))

    ar = np.arange
    g0 = MLSTM_OFF + 4 * GROUP_W
    cols_diff = np.concatenate([DIFF_OFF + perm_diff, DIFF_OFF + 512 + perm_diff, ar(DIFF_OFF + 1024, SSD_OFF)])
    cols_ret = np.concatenate([RET_OFF + perm_ret, RET_OFF + 256 + perm_ret, ar(RET_OFF + 512, DIFF_OFF)])
    cols_ssd = np.concatenate([ar(SSD_OFF, SSD_OFF + SSD_XBC),
                               np.repeat(ar(SSD_OFF + SSD_XBC, SSD_OFF + SSD_XBC + SSD_HEADS), SSD_HEAD_DIM),
                               ar(SSD_OFF + SSD_XBC + SSD_HEADS, MLSTM_OFF)])
    cols_ml = np.concatenate([ar(MLSTM_OFF, g0), np.repeat(ar(g0, g0 + 2 * MLSTM_HEADS), MLSTM_DH),
                              ar(g0 + 2 * MLSTM_HEADS, g0 + 2 * MLSTM_HEADS + GROUP_W)])
    w_diff, w_ret, w_ssd, w_ml = (jnp.take(w_in, jnp.asarray(c, jnp.int32), axis=2).astype(bf)
                                  for c in (cols_diff, cols_ret, cols_ssd, cols_ml))
    w_out_b = w_out.astype(bf)
    rep = lambda a, n: jnp.repeat(a.astype(F32), n, axis=-1)[:, None, :]
    row = lambda a: a.astype(F32)[:, None, :]
    norm_rows = row(jnp.concatenate([norm_w, final_norm_w[None]], axis=0))
    ssd_rows = [rep(a, SSD_HEAD_DIM) for a in (ssd_dt_bias, ssd_a_log, ssd_d)] + [row(ssd_norm_w)]
    ml_rows = [rep(mlstm_gate_b[:, :MLSTM_HEADS], MLSTM_DH), rep(mlstm_gate_b[:, MLSTM_HEADS:], MLSTM_DH),
               row(mlstm_norm_w)]

    u = pl.pallas_call(
        _norm_kernel, grid=(bsz, nblk), in_specs=[x_spec, layer_spec((1, dm), 0)], out_specs=x_spec,
        out_shape=u_shape, compiler_params=_params("parallel", "parallel"), name="first_norm",
    )(x, norm_rows)

    h = x
    for l in range(depth):
        nkb = seq // tq
        dqt, dk, dvt, dz = pl.pallas_call(
            _diff_proj_kernel,
            grid=(bsz, nblk),
            in_specs=[x_spec, layer_spec(w_diff.shape[1:], l), row_spec, row_spec],
            out_specs=[pl.BlockSpec((1, 2 * DIFF_HEADS, 1, LANES, ts), lambda b, s: (b, 0, s, 0, 0)),
                       grp_spec,
                       pl.BlockSpec((1, DIFF_HEADS, 1, VT_ROWS, ts), lambda b, s: (b, 0, s, 0, 0)),
                       grp_spec],
            out_shape=[jax.ShapeDtypeStruct((bsz, 2 * DIFF_HEADS, nkb, LANES, ts), bf), grp_shape,
                       jax.ShapeDtypeStruct((bsz, DIFF_HEADS, nkb, VT_ROWS, ts), bf),
                       jax.ShapeDtypeStruct((bsz, seq, GROUP_W), F32)],
            compiler_params=_params("parallel", "parallel"),
            name="diff_proj",
        )(u, w_diff, cos4, sin4)

        lam_init = 0.8 - 0.6 * math.exp(-0.3 * l)
        pair_q = pl.BlockSpec((1, tq, 256), lambda b, p, i: (b, i, p))
        diff_out = pl.pallas_call(
            functools.partial(_diff_flash_kernel, lam_init),
            grid=(bsz, 2, seq // tq),
            in_specs=[pl.BlockSpec((1, 4, 1, LANES, tq), lambda b, p, i: (b, p, i, 0, 0)),
                      pl.BlockSpec((1, seq, 256), lambda b, p, i: (b, 0, p)),
                      pl.BlockSpec((1, 2, nkb, VT_ROWS, tq), lambda b, p, i: (b, p, 0, 0, 0)),
                      pair_q, layer_spec((4, DIFF_DK), l), layer_spec((1, DIFF_DV), l)],
            out_specs=pair_q,
            out_shape=grp_shape,
            scratch_shapes=[pltpu.VMEM((4, 8, tq), F32),
                            pltpu.VMEM((4, VT_ROWS, tq), F32), pltpu.VMEM((2, tq, tq), F32),
                            pltpu.VMEM((4, tq, tq), bf)],
            compiler_params=_params("parallel", "parallel", "arbitrary"),
            name="diff_flash",
        )(dqt, dk, dvt, dz, diff_lambda.astype(F32), row(diff_norm_w))

        ret_out = pl.pallas_call(
            _ret_kernel,
            grid=(bsz, nblk),
            in_specs=[x_spec, layer_spec(w_ret.shape[1:], l), row_spec, row_spec]
                     + [_const_spec(t.shape) for t in ret_tabs],
            out_specs=grp_spec,
            out_shape=grp_shape,
            scratch_shapes=[pltpu.VMEM((256, GROUP_W), F32), pltpu.VMEM((ts, w_ret.shape[2]), F32)],
            compiler_params=_params("parallel", "arbitrary"),
            name="retention",
        )(u, w_ret, cos4, sin4, *ret_tabs)

        ssd_out = pl.pallas_call(
            _ssd_kernel,
            grid=(bsz, nblk),
            in_specs=[x_spec, layer_spec(w_ssd.shape[1:], l), layer_spec((SSD_CONV, SSD_XBC), l),
                      layer_spec((1, SSD_XBC), l)] + [layer_spec((1, GROUP_W), l)] * 4,
            out_specs=grp_spec,
            out_shape=grp_shape,
            scratch_shapes=[pltpu.VMEM((SSD_GROUPS, SSD_STATE, GROUP_W // SSD_GROUPS), F32),
                            pltpu.VMEM((ts + CONV_PAD, SSD_XBC), F32), pltpu.VMEM((ts, SSD_XBC), F32),
                            pltpu.VMEM((ts, GROUP_W), F32), pltpu.VMEM((ts, GROUP_W), F32)],
            compiler_params=_params("parallel", "arbitrary"),
            name="ssd",
        )(u, w_ssd, ssd_conv_w.astype(F32), row(ssd_conv_b), *ssd_rows)

        ml_out = pl.pallas_call(
            _mlstm_kernel,
            grid=(bsz, nblk),
            in_specs=[x_spec, layer_spec(w_ml.shape[1:], l), layer_spec((MLSTM_CONV, 2 * GROUP_W), l),
                      layer_spec((1, 2 * GROUP_W), l)] + [layer_spec((1, GROUP_W), l)] * 3,
            out_specs=grp_spec,
            out_shape=grp_shape,
            scratch_shapes=[pltpu.VMEM((MLSTM_HEADS, MLSTM_DH, MLSTM_DH), F32), pltpu.VMEM((8, MLSTM_DH), F32),
                            pltpu.VMEM((8, LANES), F32), pltpu.VMEM((ts + CONV_PAD, 2 * GROUP_W), F32),
                            pltpu.VMEM((ts, 2 * GROUP_W), F32), pltpu.VMEM((ts, w_ml.shape[2]), F32)],
            compiler_params=_params("parallel", "arbitrary"),
            name="mlstm",
        )(u, w_ml, mlstm_conv_w.astype(F32), row(mlstm_conv_b), *ml_rows)

        last = l == depth - 1
        h_shape = jax.ShapeDtypeStruct((bsz, seq, dm), F32)
        outs = pl.pallas_call(
            functools.partial(_out_kernel, last),
            grid=(bsz, nblk),
            in_specs=[x_spec] + [grp_spec] * 4 + [layer_spec((4 * GROUP_W, dm), l), layer_spec((1, dm), l + 1)],
            out_specs=[x_spec] if last else [x_spec, x_spec],
            out_shape=[h_shape] if last else [h_shape, u_shape],
            compiler_params=_params("parallel", "parallel"),
            name="out_proj",
        )(h, ret_out, diff_out, ssd_out, ml_out, w_out_b, norm_rows)
        h = outs[0]
        if not last:
            u = outs[1]
    return h
```

```python
import functools
import math

import numpy as np
import jax
import jax.numpy as jnp
from jax import lax
from jax.experimental import pallas as pl
from jax.experimental.pallas import tpu as pltpu

F32 = jnp.float32
MXU_DTYPE = jnp.bfloat16

D_MODEL = 1024
GROUP_W = 512
CHUNK = 128
ROPE_THETA = 10000.0
EPS = 1e-6
NEG_INF = -1e30
LOG2E = math.log2(math.e)

RET_HEADS, RET_DK, RET_DV = 4, 64, 128
DIFF_HEADS, DIFF_DK, DIFF_DV = 4, 64, 128
SSD_HEADS, SSD_HEAD_DIM, SSD_GROUPS, SSD_STATE, SSD_CONV = 8, 64, 2, 128, 4
SSD_XBC = GROUP_W + 2 * SSD_GROUPS * SSD_STATE
MLSTM_HEADS, MLSTM_DH, MLSTM_CONV = 4, 128, 4

RET_OFF = 0
DIFF_OFF = RET_OFF + 2 * RET_HEADS * RET_DK + 2 * GROUP_W
SSD_OFF = DIFF_OFF + 4 * GROUP_W
MLSTM_OFF = SSD_OFF + SSD_XBC + SSD_HEADS + GROUP_W

LANES = 128
CONV_PAD = 8
VT_ROWS = DIFF_DV + 16
VMEM_LIMIT = 56 * 1024 * 1024

TOKEN_BLOCK = 512


def _dot(a, b):
    return jnp.dot(a.astype(MXU_DTYPE), b.astype(MXU_DTYPE), preferred_element_type=F32)


def _dot_nt(a, b):
    return lax.dot_general(a.astype(MXU_DTYPE), b.astype(MXU_DTYPE), (((1,), (1,)), ((), ())),
                           preferred_element_type=F32)


def _dot_tn(a, b):
    return lax.dot_general(a.astype(MXU_DTYPE), b.astype(MXU_DTYPE), (((0,), (0,)), ((), ())),
                           preferred_element_type=F32)


def _sigmoid(x):
    return 1.0 / (1.0 + jnp.exp(-x))


def _silu(x):
    return x * _sigmoid(x)


def _softplus(x):
    return jnp.maximum(x, 0.0) + jnp.log1p(jnp.exp(-jnp.abs(x)))


def _rms_lanes(x):
    return x * lax.rsqrt(jnp.mean(x * x, axis=-1, keepdims=True) + EPS)


def _rope(a, cos, sin):
    x1, x2 = a[:, :LANES], a[:, LANES:]
    return jnp.concatenate([x1 * cos - x2 * sin, x2 * cos + x1 * sin], axis=-1)


def _cumsum_rows(x, tri):
    hi = x.astype(jnp.bfloat16)
    r1 = x - hi.astype(F32)
    mid = r1.astype(jnp.bfloat16)
    lo = (r1 - mid.astype(F32)).astype(jnp.bfloat16)
    mm = lambda t: jnp.dot(tri, t, preferred_element_type=F32)
    return mm(hi) + mm(mid) + mm(lo)


def _tri(n):
    row = lax.broadcasted_iota(jnp.int32, (n, n), 0)
    col = lax.broadcasted_iota(jnp.int32, (n, n), 1)
    return col <= row


def _norm_kernel(x_ref, nw_ref, u_ref):
    u_ref[0] = (_rms_lanes(x_ref[0]) * nw_ref[...]).astype(u_ref.dtype)


def _causal_conv_silu(ext_ref, raw, conv_w_ref, conv_b_ref, first_block):
    ts = raw.shape[0]

    @pl.when(first_block)
    def _():
        ext_ref[0:CONV_PAD, :] = jnp.zeros((CONV_PAD, raw.shape[1]), F32)

    ext_ref[CONV_PAD:CONV_PAD + ts, :] = raw
    taps = conv_w_ref.shape[0]
    y = conv_b_ref[...]
    for j in range(taps):
        off = CONV_PAD - (taps - 1) + j
        y = y + conv_w_ref[j:j + 1, :] * ext_ref[off:off + ts, :]
    ext_ref[0:CONV_PAD, :] = ext_ref[ts:ts + CONV_PAD, :]
    return _silu(y)


def _diff_proj_kernel(u_ref, w_ref, cos_ref, sin_ref, qt_ref, k_ref, vt_ref, z_ref):
    proj = jnp.dot(u_ref[0], w_ref[...], preferred_element_type=F32)
    ts = proj.shape[0]
    cos = cos_ref[...]
    sin = jnp.where(lax.broadcasted_iota(jnp.int32, (1, LANES), 1) < 64, -sin_ref[...], sin_ref[...])
    ones_row = jnp.where(lax.broadcasted_iota(jnp.int32, (VT_ROWS - DIFF_DV, ts), 0) == 0, 1.0, 0.0)
    half = (lax.broadcasted_iota(jnp.int32, (1, LANES), 1) % 64) // 32

    def rope(a):
        return a * cos + jnp.concatenate([a[:, 64:], a[:, :64]], axis=1) * sin

    for h in range(DIFF_HEADS):
        sl = slice(h * LANES, (h + 1) * LANES)
        q = rope(proj[:, sl]) * (DIFF_DK ** -0.5 * LOG2E)
        for t in range(2):
            qt_ref[0, 2 * h + t] = jnp.where(half == t, q, 0.0).T.astype(qt_ref.dtype)
        k_ref[0, :, sl] = rope(proj[:, 512 + h * LANES:512 + (h + 1) * LANES]).astype(k_ref.dtype)
        vt_ref[0, h, 0, 0:DIFF_DV, :] = proj[:, 1024 + h * DIFF_DV:1024 + (h + 1) * DIFF_DV].T.astype(vt_ref.dtype)
        vt_ref[0, h, 0, DIFF_DV:VT_ROWS, :] = ones_row.astype(vt_ref.dtype)
    z_ref[0] = _silu(proj[:, 1536:2048])


def _diff_flash_kernel(lam_init, qt_ref, k_ref, vt_ref, z_ref, lp_ref, nw_ref, o_ref, m_ref, acc_ref,
                       st_ref, pt_ref):
    tq = qt_ref.shape[-1]
    tk = vt_ref.shape[-1]
    i = pl.program_id(2)
    m_ref[...] = jnp.full(m_ref.shape, NEG_INF, F32)
    acc_ref[...] = jnp.zeros(acc_ref.shape, F32)

    def scores(j, g, slot):
        r0 = pl.multiple_of(j * tk, tk)
        kj = k_ref[0, pl.ds(r0, tk), (g // 2) * LANES:(g // 2 + 1) * LANES]
        st_ref[slot] = jnp.dot(kj, qt_ref[0, g], preferred_element_type=F32)

    def softmax_pv(j, g, slot, masked):
        st = st_ref[slot]
        if masked:
            keep = lax.broadcasted_iota(jnp.int32, (tk, tq), 0) <= lax.broadcasted_iota(jnp.int32, (tk, tq), 1)
            st = jnp.where(keep, st, NEG_INF)
        m_prev = m_ref[g]
        m_new = jnp.maximum(m_prev, jnp.max(st, axis=0, keepdims=True))
        pt_ref[g] = jnp.exp2(st - jnp.concatenate([m_new] * (tk // 8), axis=0)).astype(MXU_DTYPE)
        m_ref[g] = m_new
        alpha = jnp.exp2(m_prev - m_new)
        upd = jnp.dot(vt_ref[0, g // 2, j], pt_ref[g], preferred_element_type=F32)
        acc_ref[g] = jnp.concatenate([alpha] * (VT_ROWS // 8), axis=0) * acc_ref[g] + upd

    def block(j, masked):
        for g in range(4):
            if g < 3:
                scores(j, g + 1, (g + 1) % 2)
            elif not masked:
                scores(j + 1, 0, 0)
            softmax_pv(j, g, g % 2, masked)

    scores(0, 0, 0)

    def body(j, carry):
        block(j, False)
        return carry

    lax.fori_loop(0, i, body, 0)
    block(i, True)

    lp = lp_ref[...]
    lam = (jnp.exp(jnp.sum(lp[0:1] * lp[1:2], axis=-1, keepdims=True))
           - jnp.exp(jnp.sum(lp[2:3] * lp[3:4], axis=-1, keepdims=True)) + lam_init)
    for hh in range(2):
        a1, a2 = acc_ref[2 * hh], acc_ref[2 * hh + 1]
        o1 = a1[0:DIFF_DV] / a1[DIFF_DV:DIFF_DV + 1]
        o2 = a2[0:DIFF_DV] / a2[DIFF_DV:DIFF_DV + 1]
        y = _rms_lanes((o1 - lam * o2).T) * nw_ref[...] * (1.0 - lam_init)
        sl = slice(hh * LANES, (hh + 1) * LANES)
        o_ref[0, :, sl] = (y * z_ref[0, :, sl]).astype(o_ref.dtype)


def _ret_kernel(u_ref, w_ref, cos_ref, sin_ref, dec_ref, qw_ref, kw_ref, cd_ref, msk_ref,
                o_ref, st_ref, proj_ref):
    ts = u_ref.shape[1]

    @pl.when(pl.program_id(1) == 0)
    def _():
        st_ref[...] = jnp.zeros(st_ref.shape, F32)

    proj_ref[...] = jnp.dot(u_ref[0], w_ref[...], preferred_element_type=F32)
    hid = (lax.broadcasted_iota(jnp.int32, (1, 256), 1) % LANES) // 32

    def chunk(c, carry):
        r0 = pl.multiple_of(c * CHUNK, CHUNK)
        rows = pl.ds(r0, CHUNK)
        cos, sin = cos_ref[rows, :], sin_ref[rows, :]
        qr = _rope(proj_ref[rows, 0:256], cos, sin)
        kr = _rope(proj_ref[rows, 256:512], cos, sin) * (RET_DK ** -0.5)
        v = proj_ref[rows, 512:1024].astype(MXU_DTYPE)
        krb = kr.astype(MXU_DTYPE)
        state = st_ref[...]
        inter = _dot(qr * qw_ref[...], state)
        st_ref[...] = state * cd_ref[...] + _dot_tn(kr * kw_ref[...], v) * msk_ref[...]
        for h in range(RET_HEADS):
            sl = slice(h * RET_DV, (h + 1) * RET_DV)
            sc = _dot_nt(jnp.where(hid == h, qr, 0.0), krb) * dec_ref[h]
            y = _rms_lanes(_dot(sc, v[:, sl]) + inter[:, sl])
            z = proj_ref[rows, 1024 + h * RET_DV:1024 + (h + 1) * RET_DV]
            o_ref[0, rows, sl] = (y * _silu(z)).astype(o_ref.dtype)
        return carry

    lax.fori_loop(0, ts // CHUNK, chunk, 0, unroll=True)


def _ssd_kernel(u_ref, w_ref, cw_ref, cb_ref, dtb_ref, alog_ref, dskip_ref, onw_ref,
                o_ref, st_ref, ext_ref, act_ref, dt_ref, z_ref):
    ts = u_ref.shape[1]
    first = pl.program_id(1) == 0

    @pl.when(first)
    def _():
        st_ref[...] = jnp.zeros(st_ref.shape, F32)

    proj = jnp.dot(u_ref[0], w_ref[...], preferred_element_type=F32)
    act_ref[...] = _causal_conv_silu(ext_ref, proj[:, 0:SSD_XBC], cw_ref, cb_ref, first)
    dt_ref[...] = _softplus(proj[:, SSD_XBC:SSD_XBC + GROUP_W] + dtb_ref[...])
    z_ref[...] = _silu(proj[:, SSD_XBC + GROUP_W:SSD_XBC + 2 * GROUP_W])
    a_neg = -jnp.exp(alog_ref[...])
    tri = _tri(CHUNK)
    tri_b = tri.astype(jnp.bfloat16)
    lane = lax.broadcasted_iota(jnp.int32, (1, LANES), 1)
    gw = GROUP_W // SSD_GROUPS

    def chunk(c, carry):
        r0 = pl.multiple_of(c * CHUNK, CHUNK)
        rows = pl.ds(r0, CHUNK)
        xs = act_ref[rows, 0:GROUP_W]
        dt = dt_ref[rows, :]
        cs = _cumsum_rows(dt * a_neg, tri_b)
        cs_last = cs[CHUNK - 1:CHUNK, :]
        xdt = xs * dt
        y_parts = []
        for g in range(SSD_GROUPS):
            bm = act_ref[rows, GROUP_W + g * SSD_STATE:GROUP_W + (g + 1) * SSD_STATE]
            cm = act_ref[rows, GROUP_W + (SSD_GROUPS + g) * SSD_STATE:GROUP_W + (SSD_GROUPS + g + 1) * SSD_STATE]
            cb = _dot_nt(cm, bm)
            gs = slice(g * gw, (g + 1) * gw)
            prev = st_ref[g]
            y_off = _dot(cm, prev) * jnp.exp(cs[:, gs])
            st_ref[g] = prev * jnp.exp(cs_last[:, gs]) + _dot_tn(bm, xdt[:, gs] * jnp.exp(cs_last[:, gs] - cs[:, gs]))
            for pr in range(gw // LANES):
                ls = slice(g * gw + pr * LANES, g * gw + (pr + 1) * LANES)
                cs_t = cs[:, ls].T
                xdt_pair = xdt[:, ls]
                y_pair = None
                for e in range(2):
                    col = cs[:, ls][:, e * SSD_HEAD_DIM:e * SSD_HEAD_DIM + 1]
                    row = cs_t[e * SSD_HEAD_DIM:e * SSD_HEAD_DIM + 1, :]
                    lmat = jnp.exp(jnp.where(tri, col - row, -jnp.inf))
                    half = jnp.where((lane // SSD_HEAD_DIM) == e, xdt_pair, 0.0)
                    term = _dot(cb * lmat, half)
                    y_pair = term if y_pair is None else y_pair + term
                y_parts.append(y_pair + y_off[:, pr * LANES:(pr + 1) * LANES])
        y = jnp.concatenate(y_parts, axis=-1) + xs * dskip_ref[...]
        o_ref[0, rows, :] = (_rms_lanes(y * z_ref[rows, :]) * onw_ref[...]).astype(o_ref.dtype)
        return carry

    lax.fori_loop(0, ts // CHUNK, chunk, 0, unroll=True)


def _mlstm_kernel(u_ref, w_ref, cw_ref, cb_ref, ib_ref, fb_ref, onw_ref,
                  o_ref, ct_ref, n_ref, m_ref, ext_ref, act_ref, proj_ref):
    ts = u_ref.shape[1]
    first = pl.program_id(1) == 0
    nh, dh = MLSTM_HEADS, MLSTM_DH

    @pl.when(first)
    def _():
        ct_ref[...] = jnp.zeros(ct_ref.shape, F32)
        n_ref[...] = jnp.zeros(n_ref.shape, F32)
        m_ref[...] = jnp.zeros(m_ref.shape, F32)

    proj_ref[...] = jnp.dot(u_ref[0], w_ref[...], preferred_element_type=F32)
    act_ref[...] = _causal_conv_silu(ext_ref, proj_ref[:, 0:2 * GROUP_W], cw_ref, cb_ref, first)
    tri = _tri(CHUNK)
    tri_b = tri.astype(jnp.bfloat16)
    c_v, c_o, c_i, c_f, c_z = (2 * GROUP_W + k * GROUP_W for k in range(5))

    def chunk(c, carry):
        r0 = pl.multiple_of(c * CHUNK, CHUNK)
        rows = pl.ds(r0, CHUNK)
        ig = proj_ref[rows, c_i:c_i + GROUP_W] + ib_ref[...]
        fg = -_softplus(-(proj_ref[rows, c_f:c_f + GROUP_W] + fb_ref[...]))
        bcs = _cumsum_rows(fg, tri_b)
        for h in range(nh):
            sl = slice(h * dh, (h + 1) * dh)
            b_col = bcs[:, sl]
            b_row = b_col.T
            i_col = ig[:, sl]
            d_log = jnp.where(tri, b_col - b_row + i_col.T, -jnp.inf)
            m_prev = m_ref[h:h + 1, :]
            inter_log = b_col + m_prev
            m_row = jnp.maximum(jnp.max(d_log, axis=-1, keepdims=True), inter_log)
            qh = act_ref[rows, sl]
            kh = act_ref[rows, GROUP_W + h * dh:GROUP_W + (h + 1) * dh] * (dh ** -0.5)
            vh = proj_ref[rows, c_v + h * dh:c_v + (h + 1) * dh].astype(MXU_DTYPE)
            s = _dot_nt(qh, kh) * jnp.exp(d_log - m_row)
            inter_w = jnp.exp(inter_log - m_row)
            ct = ct_ref[h]
            n_row = n_ref[h:h + 1, :]
            num = _dot(s, vh) + inter_w * _dot(qh, ct)
            qn = jnp.sum(s, axis=-1, keepdims=True) + inter_w * jnp.sum(qh * n_row, axis=-1, keepdims=True)
            hv = num / jnp.maximum(jnp.abs(qn), jnp.exp(-m_row))
            b_last = b_col[CHUNK - 1:CHUNK, :]
            w_log = b_last - b_col + i_col
            m_new = jnp.maximum(b_last + m_prev, jnp.max(w_log, axis=0, keepdims=True))
            kw = kh * jnp.exp(w_log - m_new)
            decay = jnp.exp(b_last + m_prev - m_new)
            ct_ref[h] = decay * ct + _dot_tn(kw, vh)
            n_ref[h:h + 1, :] = decay * n_row + jnp.sum(kw, axis=0, keepdims=True)
            m_ref[h:h + 1, :] = m_new
            hv = _sigmoid(proj_ref[rows, c_o + h * dh:c_o + (h + 1) * dh]) * hv
            hv = _rms_lanes(hv - jnp.mean(hv, axis=-1, keepdims=True)) * onw_ref[:, sl]
            o_ref[0, rows, sl] = (hv * _silu(proj_ref[rows, c_z + h * dh:c_z + (h + 1) * dh])).astype(o_ref.dtype)
        return carry

    lax.fori_loop(0, ts // CHUNK, chunk, 0, unroll=True)


def _out_kernel(last, h_ref, a_ref, b_ref, c_ref, d_ref, w_ref, nw_ref, *o_refs):
    acc = h_ref[0]
    for k, m_ref in enumerate((a_ref, b_ref, c_ref, d_ref)):
        acc = acc + jnp.dot(m_ref[0], w_ref[k * GROUP_W:(k + 1) * GROUP_W, :], preferred_element_type=F32)
    normed = _rms_lanes(acc) * nw_ref[...]
    if last:
        o_refs[0][0] = normed
    else:
        o_refs[0][0] = acc
        o_refs[1][0] = normed.astype(o_refs[1].dtype)


def _params(*sem):
    return pltpu.CompilerParams(dimension_semantics=sem, vmem_limit_bytes=VMEM_LIMIT)


def _const_spec(shape):
    nd = len(shape)
    return pl.BlockSpec(shape, lambda *_: (0,) * nd)


def _retention_tables():
    nh, L = RET_HEADS, CHUNK
    log_g = jnp.log(1.0 - jnp.exp2(-5.0 - jnp.arange(nh, dtype=F32)))
    pos = jnp.arange(L, dtype=F32)
    rel = pos[:, None] - pos[None, :]
    decay = jnp.where(rel >= 0, jnp.exp(log_g[:, None, None] * jnp.maximum(rel, 0.0)), 0.0)
    head_of_col = (np.arange(256) % LANES) // 32
    q_w = jnp.exp(log_g[None, :] * (pos + 1.0)[:, None])[:, head_of_col]
    k_w = jnp.exp(log_g[None, :] * (L - 1.0 - pos)[:, None])[:, head_of_col]
    same_head = jnp.asarray(head_of_col[:, None] == (np.arange(GROUP_W) // RET_DV)[None, :], F32)
    chunk_decay = jnp.exp(log_g * L)[head_of_col][:, None] * same_head
    return decay, q_w, k_w, chunk_decay, same_head


def kernel(x, norm_w, w_in, w_out, diff_lambda, diff_norm_w, ssd_conv_w, ssd_conv_b, ssd_dt_bias, ssd_a_log,
           ssd_d, ssd_norm_w, mlstm_conv_w, mlstm_conv_b, mlstm_gate_b, mlstm_norm_w, final_norm_w):
    bsz, seq, dm = x.shape
    depth = w_in.shape[0]
    ts = min(TOKEN_BLOCK, seq)
    tq = ts
    nblk = seq // ts
    bf = MXU_DTYPE

    inv = ROPE_THETA ** (-jnp.arange(0, RET_DK, 2, dtype=F32) / RET_DK)
    ang = jnp.arange(seq, dtype=F32)[:, None] * inv[None, :]
    cos4, sin4 = jnp.tile(jnp.cos(ang), (1, 4)), jnp.tile(jnp.sin(ang), (1, 4))
    ret_tabs = _retention_tables()

    x_spec = pl.BlockSpec((1, ts, dm), lambda b, s: (b, s, 0))
    row_spec = pl.BlockSpec((ts, LANES), lambda b, s: (s, 0))
    grp_spec = pl.BlockSpec((1, ts, GROUP_W), lambda b, s: (b, s, 0))
    grp_shape = jax.ShapeDtypeStruct((bsz, seq, GROUP_W), bf)
    u_shape = jax.ShapeDtypeStruct((bsz, seq, dm), bf)

    def layer_spec(shape, l):
        return pl.BlockSpec((None,) + tuple(shape), lambda *_: (l,) + (0,) * len(shape))

    cols = lambda lo, n: w_in[:, :, lo:lo + n]
    expand = lambda a, n: jnp.repeat(a, n, axis=-1)
    cat = lambda parts: jnp.concatenate(parts, axis=-1).astype(bf)

    def ret_qk(lo):
        return cols(lo, 256).reshape(depth, dm, RET_HEADS, 2, 32).swapaxes(2, 3).reshape(depth, dm, 256)

    def diff_qk(lo):
        return cols(lo, 512).reshape(depth, dm, DIFF_HEADS, 2, 2, 32).swapaxes(3, 4).reshape(depth, dm, 512)

    g0 = MLSTM_OFF + 4 * GROUP_W
    w_diff = cat([diff_qk(DIFF_OFF), diff_qk(DIFF_OFF + 512), cols(DIFF_OFF + 1024, 1024)])
    w_ret = cat([ret_qk(RET_OFF), ret_qk(RET_OFF + 256), cols(RET_OFF + 512, 1024)])
    w_ssd = cat([cols(SSD_OFF, SSD_XBC), expand(cols(SSD_OFF + SSD_XBC, SSD_HEADS), SSD_HEAD_DIM),
                 cols(SSD_OFF + SSD_XBC + SSD_HEADS, GROUP_W)])
    w_ml = cat([cols(MLSTM_OFF, 4 * GROUP_W), expand(cols(g0, 2 * MLSTM_HEADS), MLSTM_DH),
                cols(g0 + 2 * MLSTM_HEADS, GROUP_W)])
    w_out_b = w_out.astype(bf)
    rep = lambda a, n: jnp.repeat(a.astype(F32), n, axis=-1)[:, None, :]
    row = lambda a: a.astype(F32)[:, None, :]
    norm_rows = row(jnp.concatenate([norm_w, final_norm_w[None]], axis=0))
    ssd_rows = [rep(a, SSD_HEAD_DIM) for a in (ssd_dt_bias, ssd_a_log, ssd_d)] + [row(ssd_norm_w)]
    ml_rows = [rep(mlstm_gate_b[:, :MLSTM_HEADS], MLSTM_DH), rep(mlstm_gate_b[:, MLSTM_HEADS:], MLSTM_DH),
               row(mlstm_norm_w)]

    u = pl.pallas_call(
        _norm_kernel, grid=(bsz, nblk), in_specs=[x_spec, layer_spec((1, dm), 0)], out_specs=x_spec,
        out_shape=u_shape, compiler_params=_params("parallel", "parallel"), name="first_norm",
    )(x, norm_rows)

    h = x
    for l in range(depth):
        nkb = nblk
        dqt, dk, dvt, dz = pl.pallas_call(
            _diff_proj_kernel,
            grid=(bsz, nblk),
            in_specs=[x_spec, layer_spec(w_diff.shape[1:], l), row_spec, row_spec],
            out_specs=[pl.BlockSpec((1, 2 * DIFF_HEADS, LANES, ts), lambda b, s: (b, 0, 0, s)),
                       grp_spec,
                       pl.BlockSpec((1, DIFF_HEADS, 1, VT_ROWS, ts), lambda b, s: (b, 0, s, 0, 0)),
                       grp_spec],
            out_shape=[jax.ShapeDtypeStruct((bsz, 2 * DIFF_HEADS, LANES, seq), bf), grp_shape,
                       jax.ShapeDtypeStruct((bsz, DIFF_HEADS, nkb, VT_ROWS, ts), bf),
                       jax.ShapeDtypeStruct((bsz, seq, GROUP_W), F32)],
            compiler_params=_params("parallel", "parallel"),
            name="diff_proj",
        )(u, w_diff, cos4, sin4)

        lam_init = 0.8 - 0.6 * math.exp(-0.3 * l)
        pair_q = pl.BlockSpec((1, tq, 256), lambda b, p, i: (b, i, p))
        diff_out = pl.pallas_call(
            functools.partial(_diff_flash_kernel, lam_init),
            grid=(bsz, 2, seq // tq),
            in_specs=[pl.BlockSpec((1, 4, LANES, tq), lambda b, p, i: (b, p, 0, i)),
                      pl.BlockSpec((1, seq, 256), lambda b, p, i: (b, 0, p)),
                      pl.BlockSpec((1, 2, nkb, VT_ROWS, ts), lambda b, p, i: (b, p, 0, 0, 0)),
                      pair_q, layer_spec((4, DIFF_DK), l), layer_spec((1, DIFF_DV), l)],
            out_specs=pair_q,
            out_shape=grp_shape,
            scratch_shapes=[pltpu.VMEM((4, 8, tq), F32),
                            pltpu.VMEM((4, VT_ROWS, tq), F32), pltpu.VMEM((2, ts, tq), F32),
                            pltpu.VMEM((4, ts, tq), bf)],
            compiler_params=_params("parallel", "parallel", "arbitrary"),
            name="diff_flash",
        )(dqt, dk, dvt, dz, diff_lambda.astype(F32), row(diff_norm_w))

        ret_out = pl.pallas_call(
            _ret_kernel,
            grid=(bsz, nblk),
            in_specs=[x_spec, layer_spec(w_ret.shape[1:], l), row_spec, row_spec]
                     + [_const_spec(t.shape) for t in ret_tabs],
            out_specs=grp_spec,
            out_shape=grp_shape,
            scratch_shapes=[pltpu.VMEM((256, GROUP_W), F32), pltpu.VMEM((ts, w_ret.shape[2]), F32)],
            compiler_params=_params("parallel", "arbitrary"),
            name="retention",
        )(u, w_ret, cos4, sin4, *ret_tabs)

        ssd_out = pl.pallas_call(
            _ssd_kernel,
            grid=(bsz, nblk),
            in_specs=[x_spec, layer_spec(w_ssd.shape[1:], l), layer_spec((SSD_CONV, SSD_XBC), l),
                      layer_spec((1, SSD_XBC), l)] + [layer_spec((1, GROUP_W), l)] * 4,
            out_specs=grp_spec,
            out_shape=grp_shape,
            scratch_shapes=[pltpu.VMEM((SSD_GROUPS, SSD_STATE, GROUP_W // SSD_GROUPS), F32),
                            pltpu.VMEM((ts + CONV_PAD, SSD_XBC), F32), pltpu.VMEM((ts, SSD_XBC), F32),
                            pltpu.VMEM((ts, GROUP_W), F32), pltpu.VMEM((ts, GROUP_W), F32)],
            compiler_params=_params("parallel", "arbitrary"),
            name="ssd",
        )(u, w_ssd, ssd_conv_w.astype(F32), row(ssd_conv_b), *ssd_rows)

        ml_out = pl.pallas_call(
            _mlstm_kernel,
            grid=(bsz, nblk),
            in_specs=[x_spec, layer_spec(w_ml.shape[1:], l), layer_spec((MLSTM_CONV, 2 * GROUP_W), l),
                      layer_spec((1, 2 * GROUP_W), l)] + [layer_spec((1, GROUP_W), l)] * 3,
            out_specs=grp_spec,
            out_shape=grp_shape,
            scratch_shapes=[pltpu.VMEM((MLSTM_HEADS, MLSTM_DH, MLSTM_DH), F32), pltpu.VMEM((8, MLSTM_DH), F32),
                            pltpu.VMEM((8, LANES), F32), pltpu.VMEM((ts + CONV_PAD, 2 * GROUP_W), F32),
                            pltpu.VMEM((ts, 2 * GROUP_W), F32), pltpu.VMEM((ts, w_ml.shape[2]), F32)],
            compiler_params=_params("parallel", "arbitrary"),
            name="mlstm",
        )(u, w_ml, mlstm_conv_w.astype(F32), row(mlstm_conv_b), *ml_rows)

        last = l == depth - 1
        h_shape = jax.ShapeDtypeStruct((bsz, seq, dm), F32)
        outs = pl.pallas_call(
            functools.partial(_out_kernel, last),
            grid=(bsz, nblk),
            in_specs=[x_spec] + [grp_spec] * 4 + [layer_spec((4 * GROUP_W, dm), l), layer_spec((1, dm), l + 1)],
            out_specs=[x_spec] if last else [x_spec, x_spec],
            out_shape=[h_shape] if last else [h_shape, u_shape],
            compiler_params=_params("parallel", "parallel"),
            name="out_proj",
        )(h, ret_out, diff_out, ssd_out, ml_out, w_out_b, norm_rows)
        h = outs[0]
        if not last:
            u = outs[1]
    return h
```

```python
import functools
import math

import numpy as np
import jax
import jax.numpy as jnp
from jax import lax
from jax.experimental import pallas as pl
from jax.experimental.pallas import tpu as pltpu

F32 = jnp.float32
MXU_DTYPE = jnp.bfloat16

D_MODEL = 1024
GROUP_W = 512
CHUNK = 128
ROPE_THETA = 10000.0
EPS = 1e-6
NEG_INF = -1e30
LOG2E = math.log2(math.e)

RET_HEADS, RET_DK, RET_DV = 4, 64, 128
DIFF_HEADS, DIFF_DK, DIFF_DV = 4, 64, 128
SSD_HEADS, SSD_HEAD_DIM, SSD_GROUPS, SSD_STATE, SSD_CONV = 8, 64, 2, 128, 4
SSD_XBC = GROUP_W + 2 * SSD_GROUPS * SSD_STATE
MLSTM_HEADS, MLSTM_DH, MLSTM_CONV = 4, 128, 4

RET_OFF = 0
DIFF_OFF = RET_OFF + 2 * RET_HEADS * RET_DK + 2 * GROUP_W
SSD_OFF = DIFF_OFF + 4 * GROUP_W
MLSTM_OFF = SSD_OFF + SSD_XBC + SSD_HEADS + GROUP_W

LANES = 128
CONV_PAD = 8
VT_ROWS = DIFF_DV + 16
VMEM_LIMIT = 56 * 1024 * 1024

TOKEN_BLOCK = 512
FLASH_HEADS_PER_STEP = 4


def _dot(a, b):
    return jnp.dot(a.astype(MXU_DTYPE), b.astype(MXU_DTYPE), preferred_element_type=F32)


def _dot_nt(a, b):
    return lax.dot_general(a.astype(MXU_DTYPE), b.astype(MXU_DTYPE), (((1,), (1,)), ((), ())),
                           preferred_element_type=F32)


def _dot_tn(a, b):
    return lax.dot_general(a.astype(MXU_DTYPE), b.astype(MXU_DTYPE), (((0,), (0,)), ((), ())),
                           preferred_element_type=F32)


def _sigmoid(x):
    return 1.0 / (1.0 + jnp.exp(-x))


def _silu(x):
    return x * _sigmoid(x)


def _softplus(x):
    return jnp.maximum(x, 0.0) + jnp.log1p(jnp.exp(-jnp.abs(x)))


def _rms_lanes(x):
    return x * lax.rsqrt(jnp.mean(x * x, axis=-1, keepdims=True) + EPS)


def _rope(a, cos, sin):
    x1, x2 = a[:, :LANES], a[:, LANES:]
    return jnp.concatenate([x1 * cos - x2 * sin, x2 * cos + x1 * sin], axis=-1)


def _cumsum_rows(x, tri):
    hi = x.astype(jnp.bfloat16)
    r1 = x - hi.astype(F32)
    mid = r1.astype(jnp.bfloat16)
    lo = (r1 - mid.astype(F32)).astype(jnp.bfloat16)
    mm = lambda t: jnp.dot(tri, t, preferred_element_type=F32)
    return mm(hi) + mm(mid) + mm(lo)


def _tri(n):
    row = lax.broadcasted_iota(jnp.int32, (n, n), 0)
    col = lax.broadcasted_iota(jnp.int32, (n, n), 1)
    return col <= row


def _norm_kernel(x_ref, nw_ref, u_ref):
    u_ref[0] = (_rms_lanes(x_ref[0]) * nw_ref[...]).astype(u_ref.dtype)


def _causal_conv_silu(ext_ref, raw, r0, conv_w_ref, conv_b_ref):
    n = raw.shape[0]
    ext_ref[CONV_PAD + r0:CONV_PAD + r0 + n, :] = raw
    taps = conv_w_ref.shape[0]
    y = conv_b_ref[...]
    for j in range(taps):
        off = CONV_PAD - (taps - 1) + j + r0
        y = y + conv_w_ref[j:j + 1, :] * ext_ref[off:off + n, :]
    return _silu(y)


def _diff_proj_kernel(u_ref, w_ref, cos_ref, sin_ref, qt_ref, k_ref, vt_ref, z_ref):
    proj = jnp.dot(u_ref[0], w_ref[...], preferred_element_type=F32)
    ts = proj.shape[0]
    cos = cos_ref[...]
    sin = jnp.where(lax.broadcasted_iota(jnp.int32, (1, LANES), 1) < 64, -sin_ref[...], sin_ref[...])
    ones_row = jnp.where(lax.broadcasted_iota(jnp.int32, (VT_ROWS - DIFF_DV, ts), 0) == 0, 1.0, 0.0)
    half = (lax.broadcasted_iota(jnp.int32, (1, LANES), 1) % 64) // 32

    def rope(a):
        return a * cos + jnp.concatenate([a[:, 64:], a[:, :64]], axis=1) * sin

    for h in range(DIFF_HEADS):
        sl = slice(h * LANES, (h + 1) * LANES)
        q = rope(proj[:, sl]) * (DIFF_DK ** -0.5 * LOG2E)
        for t in range(2):
            qt_ref[0, 2 * h + t] = jnp.where(half == t, q, 0.0).T.astype(qt_ref.dtype)
        k_ref[0, :, sl] = rope(proj[:, 512 + h * LANES:512 + (h + 1) * LANES]).astype(k_ref.dtype)
        vt_ref[0, h, 0, 0:DIFF_DV, :] = proj[:, 1024 + h * DIFF_DV:1024 + (h + 1) * DIFF_DV].T.astype(vt_ref.dtype)
        vt_ref[0, h, 0, DIFF_DV:VT_ROWS, :] = ones_row.astype(vt_ref.dtype)
    z_ref[0] = _silu(proj[:, 1536:2048])


def _diff_flash_kernel(lam_init, qt_ref, k_ref, vt_ref, z_ref, lp_ref, nw_ref, o_ref, m_ref, acc_ref,
                       st_ref, pt_ref):
    tq = qt_ref.shape[-1]
    tk = vt_ref.shape[-1]
    ng = qt_ref.shape[1]
    i = pl.program_id(2)
    m_ref[...] = jnp.full(m_ref.shape, NEG_INF, F32)
    acc_ref[...] = jnp.zeros(acc_ref.shape, F32)

    def scores(j, g, slot):
        r0 = pl.multiple_of(j * tk, tk)
        kj = k_ref[0, pl.ds(r0, tk), (g // 2) * LANES:(g // 2 + 1) * LANES]
        st_ref[slot] = jnp.dot(kj, qt_ref[0, g], preferred_element_type=F32)

    def softmax_pv(j, g, slot, masked):
        st = st_ref[slot]
        if masked:
            keep = lax.broadcasted_iota(jnp.int32, (tk, tq), 0) <= lax.broadcasted_iota(jnp.int32, (tk, tq), 1)
            st = jnp.where(keep, st, NEG_INF)
        m_prev = m_ref[g]
        m_new = jnp.maximum(m_prev, jnp.max(st, axis=0, keepdims=True))
        pt_ref[g] = jnp.exp2(st - jnp.concatenate([m_new] * (tk // 8), axis=0)).astype(MXU_DTYPE)
        m_ref[g] = m_new
        alpha = jnp.exp2(m_prev - m_new)
        upd = jnp.dot(vt_ref[0, g // 2, j], pt_ref[g], preferred_element_type=F32)
        acc_ref[g] = jnp.concatenate([alpha] * (VT_ROWS // 8), axis=0) * acc_ref[g] + upd

    def block(j, masked):
        for g in range(ng):
            if g < ng - 1:
                scores(j, g + 1, (g + 1) % 2)
            elif not masked:
                scores(j + 1, 0, 0)
            softmax_pv(j, g, g % 2, masked)

    scores(0, 0, 0)

    def body(j, carry):
        block(j, False)
        return carry

    lax.fori_loop(0, i, body, 0)
    block(i, True)

    lp = lp_ref[...]
    lam = (jnp.exp(jnp.sum(lp[0:1] * lp[1:2], axis=-1, keepdims=True))
           - jnp.exp(jnp.sum(lp[2:3] * lp[3:4], axis=-1, keepdims=True)) + lam_init)
    for hh in range(ng // 2):
        a1, a2 = acc_ref[2 * hh], acc_ref[2 * hh + 1]
        o1 = a1[0:DIFF_DV] / a1[DIFF_DV:DIFF_DV + 1]
        o2 = a2[0:DIFF_DV] / a2[DIFF_DV:DIFF_DV + 1]
        y = _rms_lanes((o1 - lam * o2).T) * nw_ref[...] * (1.0 - lam_init)
        sl = slice(hh * LANES, (hh + 1) * LANES)
        o_ref[0, :, sl] = (y * z_ref[0, :, sl]).astype(o_ref.dtype)


def _ret_kernel(u_ref, w_ref, cos_ref, sin_ref, dec_ref, qw_ref, kw_ref, cd_ref, msk_ref,
                o_ref, st_ref, proj_ref):
    ts = u_ref.shape[1]

    @pl.when(pl.program_id(1) == 0)
    def _():
        st_ref[...] = jnp.zeros(st_ref.shape, F32)

    hid = (lax.broadcasted_iota(jnp.int32, (1, 256), 1) % LANES) // 32

    def project(c):
        rows = slice(c * CHUNK, (c + 1) * CHUNK)
        proj_ref[rows, :] = jnp.dot(u_ref[0, rows, :], w_ref[...], preferred_element_type=F32)

    project(0)

    def chunk(c, carry):
        if c + 1 < ts // CHUNK:
            project(c + 1)
        rows = slice(c * CHUNK, (c + 1) * CHUNK)
        cos, sin = cos_ref[rows, :], sin_ref[rows, :]
        qr = _rope(proj_ref[rows, 0:256], cos, sin)
        kr = _rope(proj_ref[rows, 256:512], cos, sin) * (RET_DK ** -0.5)
        v = proj_ref[rows, 512:1024].astype(MXU_DTYPE)
        krb = kr.astype(MXU_DTYPE)
        state = st_ref[...]
        inter = _dot(qr * qw_ref[...], state)
        st_ref[...] = state * cd_ref[...] + _dot_tn(kr * kw_ref[...], v) * msk_ref[...]
        for h in range(RET_HEADS):
            sl = slice(h * RET_DV, (h + 1) * RET_DV)
            sc = _dot_nt(jnp.where(hid == h, qr, 0.0), krb) * dec_ref[h]
            y = _rms_lanes(_dot(sc, v[:, sl]) + inter[:, sl])
            z = proj_ref[rows, 1024 + h * RET_DV:1024 + (h + 1) * RET_DV]
            o_ref[0, rows, sl] = (y * _silu(z)).astype(o_ref.dtype)
        return carry

    for c in range(ts // CHUNK):
        chunk(c, 0)


def _ssd_kernel(u_ref, w_ref, cw_ref, cb_ref, dtb_ref, alog_ref, dskip_ref, onw_ref,
                o_ref, st_ref, ext_ref, act_ref, proj_ref):
    ts = u_ref.shape[1]
    first = pl.program_id(1) == 0

    @pl.when(first)
    def _():
        st_ref[...] = jnp.zeros(st_ref.shape, F32)
        ext_ref[0:CONV_PAD, :] = jnp.zeros((CONV_PAD, ext_ref.shape[1]), F32)

    a_neg = -jnp.exp(alog_ref[...])
    tri = _tri(CHUNK)
    tri_b = tri.astype(jnp.bfloat16)
    lane = lax.broadcasted_iota(jnp.int32, (1, LANES), 1)
    gw = GROUP_W // SSD_GROUPS

    def project(c):
        rows = slice(c * CHUNK, (c + 1) * CHUNK)
        proj_ref[rows, :] = jnp.dot(u_ref[0, rows, :], w_ref[...], preferred_element_type=F32)

    project(0)

    def chunk(c, carry):
        if c + 1 < ts // CHUNK:
            project(c + 1)
        r0 = c * CHUNK
        rows = slice(r0, r0 + CHUNK)
        act_ref[rows, :] = _causal_conv_silu(ext_ref, proj_ref[rows, 0:SSD_XBC], r0, cw_ref, cb_ref)
        dt = _softplus(proj_ref[rows, SSD_XBC:SSD_XBC + GROUP_W] + dtb_ref[...])
        z = _silu(proj_ref[rows, SSD_XBC + GROUP_W:SSD_XBC + 2 * GROUP_W])
        xs = act_ref[rows, 0:GROUP_W]
        cs = _cumsum_rows(dt * a_neg, tri_b)
        cs_last = cs[CHUNK - 1:CHUNK, :]
        xdt = xs * dt
        y_parts = []
        for g in range(SSD_GROUPS):
            bm = act_ref[rows, GROUP_W + g * SSD_STATE:GROUP_W + (g + 1) * SSD_STATE]
            cm = act_ref[rows, GROUP_W + (SSD_GROUPS + g) * SSD_STATE:GROUP_W + (SSD_GROUPS + g + 1) * SSD_STATE]
            cb = _dot_nt(cm, bm)
            gs = slice(g * gw, (g + 1) * gw)
            prev = st_ref[g]
            y_off = _dot(cm, prev) * jnp.exp(cs[:, gs])
            st_ref[g] = prev * jnp.exp(cs_last[:, gs]) + _dot_tn(bm, xdt[:, gs] * jnp.exp(cs_last[:, gs] - cs[:, gs]))
            for pr in range(gw // LANES):
                ls = slice(g * gw + pr * LANES, g * gw + (pr + 1) * LANES)
                cs_t = cs[:, ls].T
                xdt_pair = xdt[:, ls]
                y_pair = None
                for e in range(2):
                    col = cs[:, ls][:, e * SSD_HEAD_DIM:e * SSD_HEAD_DIM + 1]
                    row = cs_t[e * SSD_HEAD_DIM:e * SSD_HEAD_DIM + 1, :]
                    lmat = jnp.exp(jnp.where(tri, col - row, -jnp.inf))
                    half = jnp.where((lane // SSD_HEAD_DIM) == e, xdt_pair, 0.0)
                    term = _dot(cb * lmat, half)
                    y_pair = term if y_pair is None else y_pair + term
                y_parts.append(y_pair + y_off[:, pr * LANES:(pr + 1) * LANES])
        y = jnp.concatenate(y_parts, axis=-1) + xs * dskip_ref[...]
        o_ref[0, rows, :] = (_rms_lanes(y * z) * onw_ref[...]).astype(o_ref.dtype)
        return carry

    for c in range(ts // CHUNK):
        chunk(c, 0)
    ext_ref[0:CONV_PAD, :] = ext_ref[ts:ts + CONV_PAD, :]


def _mlstm_kernel(u_ref, w_ref, cw_ref, cb_ref, ib_ref, fb_ref, onw_ref,
                  o_ref, ct_ref, n_ref, m_ref, ext_ref, act_ref, proj_ref):
    ts = u_ref.shape[1]
    first = pl.program_id(1) == 0
    nh, dh = MLSTM_HEADS, MLSTM_DH

    @pl.when(first)
    def _():
        ct_ref[...] = jnp.zeros(ct_ref.shape, F32)
        n_ref[...] = jnp.zeros(n_ref.shape, F32)
        m_ref[...] = jnp.zeros(m_ref.shape, F32)

    @pl.when(first)
    def _():
        ext_ref[0:CONV_PAD, :] = jnp.zeros((CONV_PAD, ext_ref.shape[1]), F32)

    tri = _tri(CHUNK)
    tri_b = tri.astype(jnp.bfloat16)
    c_v, c_o, c_i, c_f, c_z = (2 * GROUP_W + k * GROUP_W for k in range(5))

    def project(c):
        rows = slice(c * CHUNK, (c + 1) * CHUNK)
        proj_ref[rows, :] = jnp.dot(u_ref[0, rows, :], w_ref[...], preferred_element_type=F32)

    project(0)

    def chunk(c, carry):
        if c + 1 < ts // CHUNK:
            project(c + 1)
        r0 = c * CHUNK
        rows = slice(r0, r0 + CHUNK)
        act_ref[rows, :] = _causal_conv_silu(ext_ref, proj_ref[rows, 0:2 * GROUP_W], r0, cw_ref, cb_ref)
        ig = proj_ref[rows, c_i:c_i + GROUP_W] + ib_ref[...]
        fg = -_softplus(-(proj_ref[rows, c_f:c_f + GROUP_W] + fb_ref[...]))
        bcs = _cumsum_rows(fg, tri_b)
        for h in range(nh):
            sl = slice(h * dh, (h + 1) * dh)
            b_col = bcs[:, sl]
            b_row = b_col.T
            i_col = ig[:, sl]
            d_log = jnp.where(tri, b_col - b_row + i_col.T, -jnp.inf)
            m_prev = m_ref[h:h + 1, :]
            inter_log = b_col + m_prev
            m_row = jnp.maximum(jnp.max(d_log, axis=-1, keepdims=True), inter_log)
            qh = act_ref[rows, sl]
            kh = act_ref[rows, GROUP_W + h * dh:GROUP_W + (h + 1) * dh] * (dh ** -0.5)
            vh = proj_ref[rows, c_v + h * dh:c_v + (h + 1) * dh].astype(MXU_DTYPE)
            s = _dot_nt(qh, kh) * jnp.exp(d_log - m_row)
            inter_w = jnp.exp(inter_log - m_row)
            ct = ct_ref[h]
            n_row = n_ref[h:h + 1, :]
            num = _dot(s, vh) + inter_w * _dot(qh, ct)
            qn = jnp.sum(s, axis=-1, keepdims=True) + inter_w * jnp.sum(qh * n_row, axis=-1, keepdims=True)
            hv = num / jnp.maximum(jnp.abs(qn), jnp.exp(-m_row))
            b_last = b_col[CHUNK - 1:CHUNK, :]
            w_log = b_last - b_col + i_col
            m_new = jnp.maximum(b_last + m_prev, jnp.max(w_log, axis=0, keepdims=True))
            kw = kh * jnp.exp(w_log - m_new)
            decay = jnp.exp(b_last + m_prev - m_new)
            ct_ref[h] = decay * ct + _dot_tn(kw, vh)
            n_ref[h:h + 1, :] = decay * n_row + jnp.sum(kw, axis=0, keepdims=True)
            m_ref[h:h + 1, :] = m_new
            hv = _sigmoid(proj_ref[rows, c_o + h * dh:c_o + (h + 1) * dh]) * hv
            hv = _rms_lanes(hv - jnp.mean(hv, axis=-1, keepdims=True)) * onw_ref[:, sl]
            o_ref[0, rows, sl] = (hv * _silu(proj_ref[rows, c_z + h * dh:c_z + (h + 1) * dh])).astype(o_ref.dtype)
        return carry

    for c in range(ts // CHUNK):
        chunk(c, 0)
    ext_ref[0:CONV_PAD, :] = ext_ref[ts:ts + CONV_PAD, :]


def _out_kernel(last, h_ref, a_ref, b_ref, c_ref, d_ref, w_ref, nw_ref, *o_refs):
    acc = h_ref[0]
    for k, m_ref in enumerate((a_ref, b_ref, c_ref, d_ref)):
        acc = acc + jnp.dot(m_ref[0], w_ref[k * GROUP_W:(k + 1) * GROUP_W, :], preferred_element_type=F32)
    normed = _rms_lanes(acc) * nw_ref[...]
    if last:
        o_refs[0][0] = normed
    else:
        o_refs[0][0] = acc
        o_refs[1][0] = normed.astype(o_refs[1].dtype)


def _params(*sem):
    return pltpu.CompilerParams(dimension_semantics=sem, vmem_limit_bytes=VMEM_LIMIT)


def _const_spec(shape):
    nd = len(shape)
    return pl.BlockSpec(shape, lambda *_: (0,) * nd)


def _retention_tables():
    nh, L = RET_HEADS, CHUNK
    log_g = jnp.log(1.0 - jnp.exp2(-5.0 - jnp.arange(nh, dtype=F32)))
    pos = jnp.arange(L, dtype=F32)
    rel = pos[:, None] - pos[None, :]
    decay = jnp.where(rel >= 0, jnp.exp(log_g[:, None, None] * jnp.maximum(rel, 0.0)), 0.0)
    head_of_col = (np.arange(256) % LANES) // 32
    q_w = jnp.exp(log_g[None, :] * (pos + 1.0)[:, None])[:, head_of_col]
    k_w = jnp.exp(log_g[None, :] * (L - 1.0 - pos)[:, None])[:, head_of_col]
    same_head = jnp.asarray(head_of_col[:, None] == (np.arange(GROUP_W) // RET_DV)[None, :], F32)
    chunk_decay = jnp.exp(log_g * L)[head_of_col][:, None] * same_head
    return decay, q_w, k_w, chunk_decay, same_head


def kernel(x, norm_w, w_in, w_out, diff_lambda, diff_norm_w, ssd_conv_w, ssd_conv_b, ssd_dt_bias, ssd_a_log,
           ssd_d, ssd_norm_w, mlstm_conv_w, mlstm_conv_b, mlstm_gate_b, mlstm_norm_w, final_norm_w):
    bsz, seq, dm = x.shape
    depth = w_in.shape[0]
    ts = min(TOKEN_BLOCK, seq)
    tq = ts
    nblk = seq // ts
    bf = MXU_DTYPE

    inv = ROPE_THETA ** (-jnp.arange(0, RET_DK, 2, dtype=F32) / RET_DK)
    ang = jnp.arange(seq, dtype=F32)[:, None] * inv[None, :]
    cos4, sin4 = jnp.tile(jnp.cos(ang), (1, 4)), jnp.tile(jnp.sin(ang), (1, 4))
    ret_tabs = _retention_tables()

    x_spec = pl.BlockSpec((1, ts, dm), lambda b, s: (b, s, 0))
    row_spec = pl.BlockSpec((ts, LANES), lambda b, s: (s, 0))
    grp_spec = pl.BlockSpec((1, ts, GROUP_W), lambda b, s: (b, s, 0))
    grp_shape = jax.ShapeDtypeStruct((bsz, seq, GROUP_W), bf)
    u_shape = jax.ShapeDtypeStruct((bsz, seq, dm), bf)

    def layer_spec(shape, l):
        return pl.BlockSpec((None,) + tuple(shape), lambda *_: (l,) + (0,) * len(shape))

    cols = lambda lo, n: w_in[:, :, lo:lo + n]
    expand = lambda a, n: jnp.repeat(a, n, axis=-1)
    cat = lambda parts: jnp.concatenate(parts, axis=-1).astype(bf)

    def ret_qk(lo):
        return cols(lo, 256).reshape(depth, dm, RET_HEADS, 2, 32).swapaxes(2, 3).reshape(depth, dm, 256)

    def diff_qk(lo):
        return cols(lo, 512).reshape(depth, dm, DIFF_HEADS, 2, 2, 32).swapaxes(3, 4).reshape(depth, dm, 512)

    g0 = MLSTM_OFF + 4 * GROUP_W
    w_diff = cat([diff_qk(DIFF_OFF), diff_qk(DIFF_OFF + 512), cols(DIFF_OFF + 1024, 1024)])
    w_ret = cat([ret_qk(RET_OFF), ret_qk(RET_OFF + 256), cols(RET_OFF + 512, 1024)])
    w_ssd = cat([cols(SSD_OFF, SSD_XBC), expand(cols(SSD_OFF + SSD_XBC, SSD_HEADS), SSD_HEAD_DIM),
                 cols(SSD_OFF + SSD_XBC + SSD_HEADS, GROUP_W)])
    w_ml = cat([cols(MLSTM_OFF, 4 * GROUP_W), expand(cols(g0, 2 * MLSTM_HEADS), MLSTM_DH),
                cols(g0 + 2 * MLSTM_HEADS, GROUP_W)])
    w_out_b = w_out.astype(bf)
    rep = lambda a, n: jnp.repeat(a.astype(F32), n, axis=-1)[:, None, :]
    row = lambda a: a.astype(F32)[:, None, :]
    norm_rows = row(jnp.concatenate([norm_w, final_norm_w[None]], axis=0))
    ssd_rows = [rep(a, SSD_HEAD_DIM) for a in (ssd_dt_bias, ssd_a_log, ssd_d)] + [row(ssd_norm_w)]
    ml_rows = [rep(mlstm_gate_b[:, :MLSTM_HEADS], MLSTM_DH), rep(mlstm_gate_b[:, MLSTM_HEADS:], MLSTM_DH),
               row(mlstm_norm_w)]

    u = pl.pallas_call(
        _norm_kernel, grid=(bsz, nblk), in_specs=[x_spec, layer_spec((1, dm), 0)], out_specs=x_spec,
        out_shape=u_shape, compiler_params=_params("parallel", "parallel"), name="first_norm",
    )(x, norm_rows)

    h = x
    for l in range(depth):
        nkb = nblk
        dqt, dk, dvt, dz = pl.pallas_call(
            _diff_proj_kernel,
            grid=(bsz, nblk),
            in_specs=[x_spec, layer_spec(w_diff.shape[1:], l), row_spec, row_spec],
            out_specs=[pl.BlockSpec((1, 2 * DIFF_HEADS, LANES, ts), lambda b, s: (b, 0, 0, s)),
                       grp_spec,
                       pl.BlockSpec((1, DIFF_HEADS, 1, VT_ROWS, ts), lambda b, s: (b, 0, s, 0, 0)),
                       grp_spec],
            out_shape=[jax.ShapeDtypeStruct((bsz, 2 * DIFF_HEADS, LANES, seq), bf), grp_shape,
                       jax.ShapeDtypeStruct((bsz, DIFF_HEADS, nkb, VT_ROWS, ts), bf),
                       jax.ShapeDtypeStruct((bsz, seq, GROUP_W), F32)],
            compiler_params=_params("parallel", "parallel"),
            name="diff_proj",
        )(u, w_diff, cos4, sin4)

        lam_init = 0.8 - 0.6 * math.exp(-0.3 * l)
        hps = FLASH_HEADS_PER_STEP
        heads_q = pl.BlockSpec((1, tq, hps * DIFF_DV), lambda b, p, i: (b, i, p))
        diff_out = pl.pallas_call(
            functools.partial(_diff_flash_kernel, lam_init),
            grid=(bsz, DIFF_HEADS // hps, seq // tq),
            in_specs=[pl.BlockSpec((1, 2 * hps, LANES, tq), lambda b, p, i: (b, p, 0, i)),
                      pl.BlockSpec((1, seq, hps * LANES), lambda b, p, i: (b, 0, p)),
                      pl.BlockSpec((1, hps, nkb, VT_ROWS, ts), lambda b, p, i: (b, p, 0, 0, 0)),
                      heads_q, layer_spec((4, DIFF_DK), l), layer_spec((1, DIFF_DV), l)],
            out_specs=heads_q,
            out_shape=grp_shape,
            scratch_shapes=[pltpu.VMEM((2 * hps, 8, tq), F32),
                            pltpu.VMEM((2 * hps, VT_ROWS, tq), F32), pltpu.VMEM((2, ts, tq), F32),
                            pltpu.VMEM((2 * hps, ts, tq), bf)],
            compiler_params=_params("parallel", "parallel", "arbitrary"),
            name="diff_flash",
        )(dqt, dk, dvt, dz, diff_lambda.astype(F32), row(diff_norm_w))

        ret_out = pl.pallas_call(
            _ret_kernel,
            grid=(bsz, nblk),
            in_specs=[x_spec, layer_spec(w_ret.shape[1:], l), row_spec, row_spec]
                     + [_const_spec(t.shape) for t in ret_tabs],
            out_specs=grp_spec,
            out_shape=grp_shape,
            scratch_shapes=[pltpu.VMEM((256, GROUP_W), F32), pltpu.VMEM((ts, w_ret.shape[2]), F32)],
            compiler_params=_params("parallel", "arbitrary"),
            name="retention",
        )(u, w_ret, cos4, sin4, *ret_tabs)

        ssd_out = pl.pallas_call(
            _ssd_kernel,
            grid=(bsz, nblk),
            in_specs=[x_spec, layer_spec(w_ssd.shape[1:], l), layer_spec((SSD_CONV, SSD_XBC), l),
                      layer_spec((1, SSD_XBC), l)] + [layer_spec((1, GROUP_W), l)] * 4,
            out_specs=grp_spec,
            out_shape=grp_shape,
            scratch_shapes=[pltpu.VMEM((SSD_GROUPS, SSD_STATE, GROUP_W // SSD_GROUPS), F32),
                            pltpu.VMEM((ts + CONV_PAD, SSD_XBC), F32), pltpu.VMEM((ts, SSD_XBC), F32),
                            pltpu.VMEM((ts, w_ssd.shape[2]), F32)],
            compiler_params=_params("parallel", "arbitrary"),
            name="ssd",
        )(u, w_ssd, ssd_conv_w.astype(F32), row(ssd_conv_b), *ssd_rows)

        ml_out = pl.pallas_call(
            _mlstm_kernel,
            grid=(bsz, nblk),
            in_specs=[x_spec, layer_spec(w_ml.shape[1:], l), layer_spec((MLSTM_CONV, 2 * GROUP_W), l),
                      layer_spec((1, 2 * GROUP_W), l)] + [layer_spec((1, GROUP_W), l)] * 3,
            out_specs=grp_spec,
            out_shape=grp_shape,
            scratch_shapes=[pltpu.VMEM((MLSTM_HEADS, MLSTM_DH, MLSTM_DH), F32), pltpu.VMEM((8, MLSTM_DH), F32),
                            pltpu.VMEM((8, LANES), F32), pltpu.VMEM((ts + CONV_PAD, 2 * GROUP_W), F32),
                            pltpu.VMEM((ts, 2 * GROUP_W), F32), pltpu.VMEM((ts, w_ml.shape[2]), F32)],
            compiler_params=_params("parallel", "arbitrary"),
            name="mlstm",
        )(u, w_ml, mlstm_conv_w.astype(F32), row(mlstm_conv_b), *ml_rows)

        last = l == depth - 1
        h_shape = jax.ShapeDtypeStruct((bsz, seq, dm), F32)
        outs = pl.pallas_call(
            functools.partial(_out_kernel, last),
            grid=(bsz, nblk),
            in_specs=[x_spec] + [grp_spec] * 4 + [layer_spec((4 * GROUP_W, dm), l), layer_spec((1, dm), l + 1)],
            out_specs=[x_spec] if last else [x_spec, x_spec],
            out_shape=[h_shape] if last else [h_shape, u_shape],
            compiler_params=_params("parallel", "parallel"),
            name="out_proj",
        )(h, ret_out, diff_out, ssd_out, ml_out, w_out_b, norm_rows)
        h = outs[0]
        if not last:
            u = outs[1]
    return h
```

```python
import functools
import math

import numpy as np
import jax
import jax.numpy as jnp
from jax import lax
from jax.experimental import pallas as pl
from jax.experimental.pallas import tpu as pltpu

F32 = jnp.float32
MXU_DTYPE = jnp.bfloat16

D_MODEL = 1024
GROUP_W = 512
CHUNK = 128
ROPE_THETA = 10000.0
EPS = 1e-6
NEG_INF = -1e30
LOG2E = math.log2(math.e)

RET_HEADS, RET_DK, RET_DV = 4, 64, 128
DIFF_HEADS, DIFF_DK, DIFF_DV = 4, 64, 128
SSD_HEADS, SSD_HEAD_DIM, SSD_GROUPS, SSD_STATE, SSD_CONV = 8, 64, 2, 128, 4
SSD_XBC = GROUP_W + 2 * SSD_GROUPS * SSD_STATE
MLSTM_HEADS, MLSTM_DH, MLSTM_CONV = 4, 128, 4

RET_OFF = 0
DIFF_OFF = RET_OFF + 2 * RET_HEADS * RET_DK + 2 * GROUP_W
SSD_OFF = DIFF_OFF + 4 * GROUP_W
MLSTM_OFF = SSD_OFF + SSD_XBC + SSD_HEADS + GROUP_W

LANES = 128
CONV_PAD = 8
VT_ROWS = DIFF_DV + 16
VMEM_LIMIT = 56 * 1024 * 1024

TOKEN_BLOCK = 512
FLASH_HEADS_PER_STEP = 4


def _dot(a, b):
    return jnp.dot(a.astype(MXU_DTYPE), b.astype(MXU_DTYPE), preferred_element_type=F32)


def _dot_nt(a, b):
    return lax.dot_general(a.astype(MXU_DTYPE), b.astype(MXU_DTYPE), (((1,), (1,)), ((), ())),
                           preferred_element_type=F32)


def _dot_tn(a, b):
    return lax.dot_general(a.astype(MXU_DTYPE), b.astype(MXU_DTYPE), (((0,), (0,)), ((), ())),
                           preferred_element_type=F32)


def _sigmoid(x):
    return 1.0 / (1.0 + jnp.exp(-x))


def _silu(x):
    return x * _sigmoid(x)


def _softplus(x):
    return jnp.maximum(x, 0.0) + jnp.log1p(jnp.exp(-jnp.abs(x)))


def _rms_lanes(x):
    return x * lax.rsqrt(jnp.mean(x * x, axis=-1, keepdims=True) + EPS)


def _rope(a, cos, sin):
    x1, x2 = a[:, :LANES], a[:, LANES:]
    return jnp.concatenate([x1 * cos - x2 * sin, x2 * cos + x1 * sin], axis=-1)


def _cumsum_rows(x, tri):
    hi = x.astype(jnp.bfloat16)
    r1 = x - hi.astype(F32)
    mid = r1.astype(jnp.bfloat16)
    lo = (r1 - mid.astype(F32)).astype(jnp.bfloat16)
    mm = lambda t: jnp.dot(tri, t, preferred_element_type=F32)
    return mm(hi) + mm(mid) + mm(lo)


def _tri(n):
    row = lax.broadcasted_iota(jnp.int32, (n, n), 0)
    col = lax.broadcasted_iota(jnp.int32, (n, n), 1)
    return col <= row


def _norm_kernel(x_ref, nw_ref, u_ref):
    u_ref[0] = (_rms_lanes(x_ref[0]) * nw_ref[...]).astype(u_ref.dtype)


def _causal_conv_silu(ext_ref, raw, r0, conv_w_ref, conv_b_ref):
    n = raw.shape[0]
    ext_ref[CONV_PAD + r0:CONV_PAD + r0 + n, :] = raw
    taps = conv_w_ref.shape[0]
    window = ext_ref[r0:r0 + CONV_PAD + n, :]
    y = conv_b_ref[...] + conv_w_ref[taps - 1:taps, :] * raw
    for j in range(taps - 1):
        y = y + conv_w_ref[j:j + 1, :] * pltpu.roll(window, taps - 1 - j, 0)[CONV_PAD:, :]
    return _silu(y)


def _diff_proj_kernel(u_ref, w_ref, cos_ref, sin_ref, qt_ref, k_ref, vt_ref, z_ref):
    proj = jnp.dot(u_ref[0], w_ref[...], preferred_element_type=F32)
    ts = proj.shape[0]
    cos = cos_ref[...]
    sin = jnp.where(lax.broadcasted_iota(jnp.int32, (1, LANES), 1) < 64, -sin_ref[...], sin_ref[...])
    ones_row = jnp.where(lax.broadcasted_iota(jnp.int32, (VT_ROWS - DIFF_DV, ts), 0) == 0, 1.0, 0.0)
    half = (lax.broadcasted_iota(jnp.int32, (1, LANES), 1) % 64) // 32

    def rope(a):
        return a * cos + jnp.concatenate([a[:, 64:], a[:, :64]], axis=1) * sin

    for h in range(DIFF_HEADS):
        sl = slice(h * LANES, (h + 1) * LANES)
        q = rope(proj[:, sl]) * (DIFF_DK ** -0.5 * LOG2E)
        for t in range(2):
            qt_ref[0, 2 * h + t] = jnp.where(half == t, q, 0.0).T.astype(qt_ref.dtype)
        k_ref[0, :, sl] = rope(proj[:, 512 + h * LANES:512 + (h + 1) * LANES]).astype(k_ref.dtype)
        vt_ref[0, h, 0, 0:DIFF_DV, :] = proj[:, 1024 + h * DIFF_DV:1024 + (h + 1) * DIFF_DV].T.astype(vt_ref.dtype)
        vt_ref[0, h, 0, DIFF_DV:VT_ROWS, :] = ones_row.astype(vt_ref.dtype)
    z_ref[0] = _silu(proj[:, 1536:2048])


def _diff_flash_kernel(lam_init, qt_ref, k_ref, vt_ref, z_ref, lp_ref, nw_ref, o_ref, m_ref, acc_ref,
                       st_ref, pt_ref):
    tq = qt_ref.shape[-1]
    tk = vt_ref.shape[-1]
    ng = qt_ref.shape[1]
    i = pl.program_id(2)
    m_ref[...] = jnp.full(m_ref.shape, NEG_INF, F32)
    acc_ref[...] = jnp.zeros(acc_ref.shape, F32)

    def scores(j, g, slot):
        r0 = pl.multiple_of(j * tk, tk)
        kj = k_ref[0, pl.ds(r0, tk), (g // 2) * LANES:(g // 2 + 1) * LANES]
        st_ref[slot] = jnp.dot(kj, qt_ref[0, g], preferred_element_type=F32)

    def softmax_pv(j, g, slot, masked):
        st = st_ref[slot]
        if masked:
            keep = lax.broadcasted_iota(jnp.int32, (tk, tq), 0) <= lax.broadcasted_iota(jnp.int32, (tk, tq), 1)
            st = jnp.where(keep, st, NEG_INF)
        m_prev = m_ref[g]
        m_new = jnp.maximum(m_prev, jnp.max(st, axis=0, keepdims=True))
        pt_ref[g] = jnp.exp2(st - jnp.concatenate([m_new] * (tk // 8), axis=0)).astype(MXU_DTYPE)
        m_ref[g] = m_new
        alpha = jnp.exp2(m_prev - m_new)
        upd = jnp.dot(vt_ref[0, g // 2, j], pt_ref[g], preferred_element_type=F32)
        acc_ref[g] = jnp.concatenate([alpha] * (VT_ROWS // 8), axis=0) * acc_ref[g] + upd

    def block(j, masked):
        for g in range(ng):
            if g < ng - 1:
                scores(j, g + 1, (g + 1) % 2)
            elif not masked:
                scores(j + 1, 0, 0)
            softmax_pv(j, g, g % 2, masked)

    scores(0, 0, 0)

    def body(j, carry):
        block(j, False)
        return carry

    lax.fori_loop(0, i, body, 0)
    block(i, True)

    lp = lp_ref[...]
    lam = (jnp.exp(jnp.sum(lp[0:1] * lp[1:2], axis=-1, keepdims=True))
           - jnp.exp(jnp.sum(lp[2:3] * lp[3:4], axis=-1, keepdims=True)) + lam_init)
    for hh in range(ng // 2):
        a1, a2 = acc_ref[2 * hh], acc_ref[2 * hh + 1]
        o1 = a1[0:DIFF_DV] / a1[DIFF_DV:DIFF_DV + 1]
        o2 = a2[0:DIFF_DV] / a2[DIFF_DV:DIFF_DV + 1]
        y = _rms_lanes((o1 - lam * o2).T) * nw_ref[...] * (1.0 - lam_init)
        sl = slice(hh * LANES, (hh + 1) * LANES)
        o_ref[0, :, sl] = (y * z_ref[0, :, sl]).astype(o_ref.dtype)


def _ret_kernel(u_ref, w_ref, cos_ref, sin_ref, dec_ref, qw_ref, kw_ref, cd_ref, msk_ref,
                o_ref, st_ref, proj_ref):
    ts = u_ref.shape[1]

    @pl.when(pl.program_id(1) == 0)
    def _():
        st_ref[...] = jnp.zeros(st_ref.shape, F32)

    hid = (lax.broadcasted_iota(jnp.int32, (1, 256), 1) % LANES) // 32

    def project(c):
        rows = slice(c * CHUNK, (c + 1) * CHUNK)
        proj_ref[rows, :] = jnp.dot(u_ref[0, rows, :], w_ref[...], preferred_element_type=F32)

    project(0)

    def chunk(c, carry):
        if c + 1 < ts // CHUNK:
            project(c + 1)
        rows = slice(c * CHUNK, (c + 1) * CHUNK)
        cos, sin = cos_ref[rows, :], sin_ref[rows, :]
        qr = _rope(proj_ref[rows, 0:256], cos, sin)
        kr = _rope(proj_ref[rows, 256:512], cos, sin) * (RET_DK ** -0.5)
        v = proj_ref[rows, 512:1024].astype(MXU_DTYPE)
        krb = kr.astype(MXU_DTYPE)
        state = st_ref[...]
        inter = _dot(qr * qw_ref[...], state)
        st_ref[...] = state * cd_ref[...] + _dot_tn(kr * kw_ref[...], v) * msk_ref[...]
        for h in range(RET_HEADS):
            sl = slice(h * RET_DV, (h + 1) * RET_DV)
            sc = _dot_nt(jnp.where(hid == h, qr, 0.0), krb) * dec_ref[h]
            y = _rms_lanes(_dot(sc, v[:, sl]) + inter[:, sl])
            z = proj_ref[rows, 1024 + h * RET_DV:1024 + (h + 1) * RET_DV]
            o_ref[0, rows, sl] = (y * _silu(z)).astype(o_ref.dtype)
        return carry

    for c in range(ts // CHUNK):
        chunk(c, 0)


def _ssd_kernel(u_ref, w_ref, cw_ref, cb_ref, dtb_ref, alog_ref, dskip_ref, onw_ref,
                o_ref, st_ref, ext_ref, act_ref, proj_ref):
    ts = u_ref.shape[1]
    first = pl.program_id(1) == 0

    @pl.when(first)
    def _():
        st_ref[...] = jnp.zeros(st_ref.shape, F32)
        ext_ref[0:CONV_PAD, :] = jnp.zeros((CONV_PAD, ext_ref.shape[1]), F32)

    a_neg = -jnp.exp(alog_ref[...])
    tri = _tri(CHUNK)
    tri_b = tri.astype(jnp.bfloat16)
    lane = lax.broadcasted_iota(jnp.int32, (1, LANES), 1)
    gw = GROUP_W // SSD_GROUPS

    def project(c):
        rows = slice(c * CHUNK, (c + 1) * CHUNK)
        proj_ref[rows, :] = jnp.dot(u_ref[0, rows, :], w_ref[...], preferred_element_type=F32)

    project(0)

    def chunk(c, carry):
        if c + 1 < ts // CHUNK:
            project(c + 1)
        r0 = c * CHUNK
        rows = slice(r0, r0 + CHUNK)
        act_ref[rows, :] = _causal_conv_silu(ext_ref, proj_ref[rows, 0:SSD_XBC], r0, cw_ref, cb_ref)
        dt = _softplus(proj_ref[rows, SSD_XBC:SSD_XBC + GROUP_W] + dtb_ref[...])
        z = _silu(proj_ref[rows, SSD_XBC + GROUP_W:SSD_XBC + 2 * GROUP_W])
        xs = act_ref[rows, 0:GROUP_W]
        cs = _cumsum_rows(dt * a_neg, tri_b)
        cs_last = cs[CHUNK - 1:CHUNK, :]
        xdt = xs * dt
        y_parts = []
        for g in range(SSD_GROUPS):
            bm = act_ref[rows, GROUP_W + g * SSD_STATE:GROUP_W + (g + 1) * SSD_STATE]
            cm = act_ref[rows, GROUP_W + (SSD_GROUPS + g) * SSD_STATE:GROUP_W + (SSD_GROUPS + g + 1) * SSD_STATE]
            cb = _dot_nt(cm, bm)
            gs = slice(g * gw, (g + 1) * gw)
            prev = st_ref[g]
            y_off = _dot(cm, prev) * jnp.exp(cs[:, gs])
            st_ref[g] = prev * jnp.exp(cs_last[:, gs]) + _dot_tn(bm, xdt[:, gs] * jnp.exp(cs_last[:, gs] - cs[:, gs]))
            for pr in range(gw // LANES):
                ls = slice(g * gw + pr * LANES, g * gw + (pr + 1) * LANES)
                cs_t = cs[:, ls].T
                xdt_pair = xdt[:, ls]
                y_pair = None
                for e in range(2):
                    col = cs[:, ls][:, e * SSD_HEAD_DIM:e * SSD_HEAD_DIM + 1]
                    row = cs_t[e * SSD_HEAD_DIM:e * SSD_HEAD_DIM + 1, :]
                    lmat = jnp.exp(jnp.where(tri, col - row, -jnp.inf))
                    half = jnp.where((lane // SSD_HEAD_DIM) == e, xdt_pair, 0.0)
                    term = _dot(cb * lmat, half)
                    y_pair = term if y_pair is None else y_pair + term
                y_parts.append(y_pair + y_off[:, pr * LANES:(pr + 1) * LANES])
        y = jnp.concatenate(y_parts, axis=-1) + xs * dskip_ref[...]
        o_ref[0, rows, :] = (_rms_lanes(y * z) * onw_ref[...]).astype(o_ref.dtype)
        return carry

    for c in range(ts // CHUNK):
        chunk(c, 0)
    ext_ref[0:CONV_PAD, :] = ext_ref[ts:ts + CONV_PAD, :]


def _mlstm_kernel(u_ref, w_ref, cw_ref, cb_ref, ib_ref, fb_ref, onw_ref,
                  o_ref, ct_ref, n_ref, m_ref, ext_ref, act_ref, proj_ref):
    ts = u_ref.shape[1]
    first = pl.program_id(1) == 0
    nh, dh = MLSTM_HEADS, MLSTM_DH

    @pl.when(first)
    def _():
        ct_ref[...] = jnp.zeros(ct_ref.shape, F32)
        n_ref[...] = jnp.zeros(n_ref.shape, F32)
        m_ref[...] = jnp.zeros(m_ref.shape, F32)

    @pl.when(first)
    def _():
        ext_ref[0:CONV_PAD, :] = jnp.zeros((CONV_PAD, ext_ref.shape[1]), F32)

    tri = _tri(CHUNK)
    tri_b = tri.astype(jnp.bfloat16)
    c_v, c_o, c_i, c_f, c_z = (2 * GROUP_W + k * GROUP_W for k in range(5))

    def project(c):
        rows = slice(c * CHUNK, (c + 1) * CHUNK)
        proj_ref[rows, :] = jnp.dot(u_ref[0, rows, :], w_ref[...], preferred_element_type=F32)

    project(0)

    def chunk(c, carry):
        if c + 1 < ts // CHUNK:
            project(c + 1)
        r0 = c * CHUNK
        rows = slice(r0, r0 + CHUNK)
        act_ref[rows, :] = _causal_conv_silu(ext_ref, proj_ref[rows, 0:2 * GROUP_W], r0, cw_ref, cb_ref)
        ig = proj_ref[rows, c_i:c_i + GROUP_W] + ib_ref[...]
        fg = -_softplus(-(proj_ref[rows, c_f:c_f + GROUP_W] + fb_ref[...]))
        bcs = _cumsum_rows(fg, tri_b)
        for h in range(nh):
            sl = slice(h * dh, (h + 1) * dh)
            b_col = bcs[:, sl]
            b_row = b_col.T
            i_col = ig[:, sl]
            d_log = jnp.where(tri, b_col - b_row + i_col.T, -jnp.inf)
            m_prev = m_ref[h:h + 1, :]
            inter_log = b_col + m_prev
            m_row = jnp.maximum(jnp.max(d_log, axis=-1, keepdims=True), inter_log)
            qh = act_ref[rows, sl]
            kh = act_ref[rows, GROUP_W + h * dh:GROUP_W + (h + 1) * dh] * (dh ** -0.5)
            vh = proj_ref[rows, c_v + h * dh:c_v + (h + 1) * dh].astype(MXU_DTYPE)
            s = _dot_nt(qh, kh) * jnp.exp(d_log - m_row)
            inter_w = jnp.exp(inter_log - m_row)
            ct = ct_ref[h]
            n_row = n_ref[h:h + 1, :]
            num = _dot(s, vh) + inter_w * _dot(qh, ct)
            qn = jnp.sum(s, axis=-1, keepdims=True) + inter_w * jnp.sum(qh * n_row, axis=-1, keepdims=True)
            hv = num / jnp.maximum(jnp.abs(qn), jnp.exp(-m_row))
            b_last = b_col[CHUNK - 1:CHUNK, :]
            w_log = b_last - b_col + i_col
            m_new = jnp.maximum(b_last + m_prev, jnp.max(w_log, axis=0, keepdims=True))
            kw = kh * jnp.exp(w_log - m_new)
            decay = jnp.exp(b_last + m_prev - m_new)
            ct_ref[h] = decay * ct + _dot_tn(kw, vh)
            n_ref[h:h + 1, :] = decay * n_row + jnp.sum(kw, axis=0, keepdims=True)
            m_ref[h:h + 1, :] = m_new
            hv = _sigmoid(proj_ref[rows, c_o + h * dh:c_o + (h + 1) * dh]) * hv
            hv = _rms_lanes(hv - jnp.mean(hv, axis=-1, keepdims=True)) * onw_ref[:, sl]
            o_ref[0, rows, sl] = (hv * _silu(proj_ref[rows, c_z + h * dh:c_z + (h + 1) * dh])).astype(o_ref.dtype)
        return carry

    for c in range(ts // CHUNK):
        chunk(c, 0)
    ext_ref[0:CONV_PAD, :] = ext_ref[ts:ts + CONV_PAD, :]


def _out_kernel(last, h_ref, a_ref, b_ref, c_ref, d_ref, w_ref, nw_ref, *o_refs):
    acc = h_ref[0]
    for k, m_ref in enumerate((a_ref, b_ref, c_ref, d_ref)):
        acc = acc + jnp.dot(m_ref[0], w_ref[k * GROUP_W:(k + 1) * GROUP_W, :], preferred_element_type=F32)
    normed = _rms_lanes(acc) * nw_ref[...]
    if last:
        o_refs[0][0] = normed
    else:
        o_refs[0][0] = acc
        o_refs[1][0] = normed.astype(o_refs[1].dtype)


def _params(*sem):
    return pltpu.CompilerParams(dimension_semantics=sem, vmem_limit_bytes=VMEM_LIMIT)


def _const_spec(shape):
    nd = len(shape)
    return pl.BlockSpec(shape, lambda *_: (0,) * nd)


def _retention_tables():
    nh, L = RET_HEADS, CHUNK
    log_g = jnp.log(1.0 - jnp.exp2(-5.0 - jnp.arange(nh, dtype=F32)))
    pos = jnp.arange(L, dtype=F32)
    rel = pos[:, None] - pos[None, :]
    decay = jnp.where(rel >= 0, jnp.exp(log_g[:, None, None] * jnp.maximum(rel, 0.0)), 0.0)
    head_of_col = (np.arange(256) % LANES) // 32
    q_w = jnp.exp(log_g[None, :] * (pos + 1.0)[:, None])[:, head_of_col]
    k_w = jnp.exp(log_g[None, :] * (L - 1.0 - pos)[:, None])[:, head_of_col]
    same_head = jnp.asarray(head_of_col[:, None] == (np.arange(GROUP_W) // RET_DV)[None, :], F32)
    chunk_decay = jnp.exp(log_g * L)[head_of_col][:, None] * same_head
    return decay, q_w, k_w, chunk_decay, same_head


def kernel(x, norm_w, w_in, w_out, diff_lambda, diff_norm_w, ssd_conv_w, ssd_conv_b, ssd_dt_bias, ssd_a_log,
           ssd_d, ssd_norm_w, mlstm_conv_w, mlstm_conv_b, mlstm_gate_b, mlstm_norm_w, final_norm_w):
    bsz, seq, dm = x.shape
    depth = w_in.shape[0]
    ts = min(TOKEN_BLOCK, seq)
    tq = ts
    nblk = seq // ts
    bf = MXU_DTYPE

    inv = ROPE_THETA ** (-jnp.arange(0, RET_DK, 2, dtype=F32) / RET_DK)
    ang = jnp.arange(seq, dtype=F32)[:, None] * inv[None, :]
    cos4, sin4 = jnp.tile(jnp.cos(ang), (1, 4)), jnp.tile(jnp.sin(ang), (1, 4))
    ret_tabs = _retention_tables()

    x_spec = pl.BlockSpec((1, ts, dm), lambda b, s: (b, s, 0))
    row_spec = pl.BlockSpec((ts, LANES), lambda b, s: (s, 0))
    grp_spec = pl.BlockSpec((1, ts, GROUP_W), lambda b, s: (b, s, 0))
    grp_shape = jax.ShapeDtypeStruct((bsz, seq, GROUP_W), bf)
    u_shape = jax.ShapeDtypeStruct((bsz, seq, dm), bf)

    def layer_spec(shape, l):
        return pl.BlockSpec((None,) + tuple(shape), lambda *_: (l,) + (0,) * len(shape))

    cols = lambda lo, n: w_in[:, :, lo:lo + n]
    expand = lambda a, n: jnp.repeat(a, n, axis=-1)
    cat = lambda parts: jnp.concatenate(parts, axis=-1).astype(bf)

    def ret_qk(lo):
        return cols(lo, 256).reshape(depth, dm, RET_HEADS, 2, 32).swapaxes(2, 3).reshape(depth, dm, 256)

    def diff_qk(lo):
        return cols(lo, 512).reshape(depth, dm, DIFF_HEADS, 2, 2, 32).swapaxes(3, 4).reshape(depth, dm, 512)

    g0 = MLSTM_OFF + 4 * GROUP_W
    w_diff = cat([diff_qk(DIFF_OFF), diff_qk(DIFF_OFF + 512), cols(DIFF_OFF + 1024, 1024)])
    w_ret = cat([ret_qk(RET_OFF), ret_qk(RET_OFF + 256), cols(RET_OFF + 512, 1024)])
    w_ssd = cat([cols(SSD_OFF, SSD_XBC), expand(cols(SSD_OFF + SSD_XBC, SSD_HEADS), SSD_HEAD_DIM),
                 cols(SSD_OFF + SSD_XBC + SSD_HEADS, GROUP_W)])
    w_ml = cat([cols(MLSTM_OFF, 4 * GROUP_W), expand(cols(g0, 2 * MLSTM_HEADS), MLSTM_DH),
                cols(g0 + 2 * MLSTM_HEADS, GROUP_W)])
    w_out_b = w_out.astype(bf)
    rep = lambda a, n: jnp.repeat(a.astype(F32), n, axis=-1)[:, None, :]
    row = lambda a: a.astype(F32)[:, None, :]
    norm_rows = row(jnp.concatenate([norm_w, final_norm_w[None]], axis=0))
    ssd_rows = [rep(a, SSD_HEAD_DIM) for a in (ssd_dt_bias, ssd_a_log, ssd_d)] + [row(ssd_norm_w)]
    ml_rows = [rep(mlstm_gate_b[:, :MLSTM_HEADS], MLSTM_DH), rep(mlstm_gate_b[:, MLSTM_HEADS:], MLSTM_DH),
               row(mlstm_norm_w)]

    u = pl.pallas_call(
        _norm_kernel, grid=(bsz, nblk), in_specs=[x_spec, layer_spec((1, dm), 0)], out_specs=x_spec,
        out_shape=u_shape, compiler_params=_params("parallel", "parallel"), name="first_norm",
    )(x, norm_rows)

    h = x
    for l in range(depth):
        nkb = nblk
        dqt, dk, dvt, dz = pl.pallas_call(
            _diff_proj_kernel,
            grid=(bsz, nblk),
            in_specs=[x_spec, layer_spec(w_diff.shape[1:], l), row_spec, row_spec],
            out_specs=[pl.BlockSpec((1, 2 * DIFF_HEADS, LANES, ts), lambda b, s: (b, 0, 0, s)),
                       grp_spec,
                       pl.BlockSpec((1, DIFF_HEADS, 1, VT_ROWS, ts), lambda b, s: (b, 0, s, 0, 0)),
                       grp_spec],
            out_shape=[jax.ShapeDtypeStruct((bsz, 2 * DIFF_HEADS, LANES, seq), bf), grp_shape,
                       jax.ShapeDtypeStruct((bsz, DIFF_HEADS, nkb, VT_ROWS, ts), bf),
                       jax.ShapeDtypeStruct((bsz, seq, GROUP_W), F32)],
            compiler_params=_params("parallel", "parallel"),
            name="diff_proj",
        )(u, w_diff, cos4, sin4)

        lam_init = 0.8 - 0.6 * math.exp(-0.3 * l)
        hps = FLASH_HEADS_PER_STEP
        heads_q = pl.BlockSpec((1, tq, hps * DIFF_DV), lambda b, p, i: (b, i, p))
        diff_out = pl.pallas_call(
            functools.partial(_diff_flash_kernel, lam_init),
            grid=(bsz, DIFF_HEADS // hps, seq // tq),
            in_specs=[pl.BlockSpec((1, 2 * hps, LANES, tq), lambda b, p, i: (b, p, 0, i)),
                      pl.BlockSpec((1, seq, hps * LANES), lambda b, p, i: (b, 0, p)),
                      pl.BlockSpec((1, hps, nkb, VT_ROWS, ts), lambda b, p, i: (b, p, 0, 0, 0)),
                      heads_q, layer_spec((4, DIFF_DK), l), layer_spec((1, DIFF_DV), l)],
            out_specs=heads_q,
            out_shape=grp_shape,
            scratch_shapes=[pltpu.VMEM((2 * hps, 8, tq), F32),
                            pltpu.VMEM((2 * hps, VT_ROWS, tq), F32), pltpu.VMEM((2, ts, tq), F32),
                            pltpu.VMEM((2 * hps, ts, tq), bf)],
            compiler_params=_params("parallel", "parallel", "arbitrary"),
            name="diff_flash",
        )(dqt, dk, dvt, dz, diff_lambda.astype(F32), row(diff_norm_w))

        ret_out = pl.pallas_call(
            _ret_kernel,
            grid=(bsz, nblk),
            in_specs=[x_spec, layer_spec(w_ret.shape[1:], l), row_spec, row_spec]
                     + [_const_spec(t.shape) for t in ret_tabs],
            out_specs=grp_spec,
            out_shape=grp_shape,
            scratch_shapes=[pltpu.VMEM((256, GROUP_W), F32), pltpu.VMEM((ts, w_ret.shape[2]), F32)],
            compiler_params=_params("parallel", "arbitrary"),
            name="retention",
        )(u, w_ret, cos4, sin4, *ret_tabs)

        ssd_out = pl.pallas_call(
            _ssd_kernel,
            grid=(bsz, nblk),
            in_specs=[x_spec, layer_spec(w_ssd.shape[1:], l), layer_spec((SSD_CONV, SSD_XBC), l),
                      layer_spec((1, SSD_XBC), l)] + [layer_spec((1, GROUP_W), l)] * 4,
            out_specs=grp_spec,
            out_shape=grp_shape,
            scratch_shapes=[pltpu.VMEM((SSD_GROUPS, SSD_STATE, GROUP_W // SSD_GROUPS), F32),
                            pltpu.VMEM((ts + CONV_PAD, SSD_XBC), F32), pltpu.VMEM((ts, SSD_XBC), F32),
                            pltpu.VMEM((ts, w_ssd.shape[2]), F32)],
            compiler_params=_params("parallel", "arbitrary"),
            name="ssd",
        )(u, w_ssd, ssd_conv_w.astype(F32), row(ssd_conv_b), *ssd_rows)

        ml_out = pl.pallas_call(
            _mlstm_kernel,
            grid=(bsz, nblk),
            in_specs=[x_spec, layer_spec(w_ml.shape[1:], l), layer_spec((MLSTM_CONV, 2 * GROUP_W), l),
                      layer_spec((1, 2 * GROUP_W), l)] + [layer_spec((1, GROUP_W), l)] * 3,
            out_specs=grp_spec,
            out_shape=grp_shape,
            scratch_shapes=[pltpu.VMEM((MLSTM_HEADS, MLSTM_DH, MLSTM_DH), F32), pltpu.VMEM((8, MLSTM_DH), F32),
                            pltpu.VMEM((8, LANES), F32), pltpu.VMEM((ts + CONV_PAD, 2 * GROUP_W), F32),
                            pltpu.VMEM((ts, 2 * GROUP_W), F32), pltpu.VMEM((ts, w_ml.shape[2]), F32)],
            compiler_params=_params("parallel", "arbitrary"),
            name="mlstm",
        )(u, w_ml, mlstm_conv_w.astype(F32), row(mlstm_conv_b), *ml_rows)

        last = l == depth - 1
        h_shape = jax.ShapeDtypeStruct((bsz, seq, dm), F32)
        outs = pl.pallas_call(
            functools.partial(_out_kernel, last),
            grid=(bsz, nblk),
            in_specs=[x_spec] + [grp_spec] * 4 + [layer_spec((4 * GROUP_W, dm), l), layer_spec((1, dm), l + 1)],
            out_specs=[x_spec] if last else [x_spec, x_spec],
            out_shape=[h_shape] if last else [h_shape, u_shape],
            compiler_params=_params("parallel", "parallel"),
            name="out_proj",
        )(h, ret_out, diff_out, ssd_out, ml_out, w_out_b, norm_rows)
        h = outs[0]
        if not last:
            u = outs[1]
    return h
```

```python
import functools
import math

import numpy as np
import jax
import jax.numpy as jnp
from jax import lax
from jax.experimental import pallas as pl
from jax.experimental.pallas import tpu as pltpu

F32 = jnp.float32
MXU_DTYPE = jnp.bfloat16

D_MODEL = 1024
GROUP_W = 512
CHUNK = 128
ROPE_THETA = 10000.0
EPS = 1e-6
NEG_INF = -1e30
LOG2E = math.log2(math.e)

RET_HEADS, RET_DK, RET_DV = 4, 64, 128
DIFF_HEADS, DIFF_DK, DIFF_DV = 4, 64, 128
SSD_HEADS, SSD_HEAD_DIM, SSD_GROUPS, SSD_STATE, SSD_CONV = 8, 64, 2, 128, 4
SSD_XBC = GROUP_W + 2 * SSD_GROUPS * SSD_STATE
MLSTM_HEADS, MLSTM_DH, MLSTM_CONV = 4, 128, 4

RET_OFF = 0
DIFF_OFF = RET_OFF + 2 * RET_HEADS * RET_DK + 2 * GROUP_W
SSD_OFF = DIFF_OFF + 4 * GROUP_W
MLSTM_OFF = SSD_OFF + SSD_XBC + SSD_HEADS + GROUP_W

LANES = 128
CONV_PAD = 8
VT_ROWS = DIFF_DV + 16
VMEM_LIMIT = 56 * 1024 * 1024

TOKEN_BLOCK = 512
FLASH_HEADS_PER_STEP = 4


def _dot(a, b):
    return jnp.dot(a.astype(MXU_DTYPE), b.astype(MXU_DTYPE), preferred_element_type=F32)


def _dot_nt(a, b):
    return lax.dot_general(a.astype(MXU_DTYPE), b.astype(MXU_DTYPE), (((1,), (1,)), ((), ())),
                           preferred_element_type=F32)


def _dot_tn(a, b):
    return lax.dot_general(a.astype(MXU_DTYPE), b.astype(MXU_DTYPE), (((0,), (0,)), ((), ())),
                           preferred_element_type=F32)


def _sigmoid(x):
    return 1.0 / (1.0 + jnp.exp(-x))


def _silu(x):
    return x * _sigmoid(x)


def _softplus(x):
    return jnp.maximum(x, 0.0) + jnp.log1p(jnp.exp(-jnp.abs(x)))


def _rms_lanes(x):
    return x * lax.rsqrt(jnp.mean(x * x, axis=-1, keepdims=True) + EPS)


def _rope(a, cos, sin):
    x1, x2 = a[:, :LANES], a[:, LANES:]
    return jnp.concatenate([x1 * cos - x2 * sin, x2 * cos + x1 * sin], axis=-1)


def _cumsum_rows(x, tri):
    hi = x.astype(jnp.bfloat16)
    r1 = x - hi.astype(F32)
    mid = r1.astype(jnp.bfloat16)
    lo = (r1 - mid.astype(F32)).astype(jnp.bfloat16)
    mm = lambda t: jnp.dot(tri, t, preferred_element_type=F32)
    return mm(hi) + mm(mid) + mm(lo)


def _tri(n):
    row = lax.broadcasted_iota(jnp.int32, (n, n), 0)
    col = lax.broadcasted_iota(jnp.int32, (n, n), 1)
    return col <= row


def _norm_kernel(x_ref, nw_ref, u_ref):
    u_ref[0] = (_rms_lanes(x_ref[0]) * nw_ref[...]).astype(u_ref.dtype)


def _causal_conv_silu(ext_ref, raw, r0, conv_w_ref, conv_b_ref):
    n = raw.shape[0]
    ext_ref[CONV_PAD + r0:CONV_PAD + r0 + n, :] = raw
    taps = conv_w_ref.shape[0]
    window = ext_ref[r0:r0 + CONV_PAD + n, :]
    y = conv_b_ref[...] + conv_w_ref[taps - 1:taps, :] * raw
    for j in range(taps - 1):
        y = y + conv_w_ref[j:j + 1, :] * pltpu.roll(window, taps - 1 - j, 0)[CONV_PAD:, :]
    return _silu(y)


def _diff_proj_kernel(u_ref, w_ref, cos_ref, sin_ref, qt_ref, k_ref, vt_ref, z_ref):
    proj = jnp.dot(u_ref[0], w_ref[...], preferred_element_type=F32)
    ts = proj.shape[0]
    cos = cos_ref[...]
    sin = jnp.where(lax.broadcasted_iota(jnp.int32, (1, LANES), 1) < 64, -sin_ref[...], sin_ref[...])
    ones_row = jnp.where(lax.broadcasted_iota(jnp.int32, (VT_ROWS - DIFF_DV, ts), 0) == 0, 1.0, 0.0)
    half = (lax.broadcasted_iota(jnp.int32, (1, LANES), 1) % 64) // 32

    def rope(a):
        return a * cos + jnp.concatenate([a[:, 64:], a[:, :64]], axis=1) * sin

    for h in range(DIFF_HEADS):
        sl = slice(h * LANES, (h + 1) * LANES)
        q = rope(proj[:, sl]) * (DIFF_DK ** -0.5 * LOG2E)
        for t in range(2):
            qt_ref[0, 2 * h + t] = jnp.where(half == t, q, 0.0).T.astype(qt_ref.dtype)
        k_ref[0, :, sl] = rope(proj[:, 512 + h * LANES:512 + (h + 1) * LANES]).astype(k_ref.dtype)
        vt_ref[0, h, 0, 0:DIFF_DV, :] = proj[:, 1024 + h * DIFF_DV:1024 + (h + 1) * DIFF_DV].T.astype(vt_ref.dtype)
        vt_ref[0, h, 0, DIFF_DV:VT_ROWS, :] = ones_row.astype(vt_ref.dtype)
    z_ref[0] = _silu(proj[:, 1536:2048])


def _diff_flash_kernel(lam_init, qt_ref, k_ref, vt_ref, z_ref, lp_ref, nw_ref, o_ref, m_ref, acc_ref,
                       st_ref, pt_ref):
    tq = qt_ref.shape[-1]
    tk = vt_ref.shape[-1]
    ng = qt_ref.shape[1]
    i = pl.program_id(2)
    m_ref[...] = jnp.full(m_ref.shape, NEG_INF, F32)
    acc_ref[...] = jnp.zeros(acc_ref.shape, F32)

    def scores(j, g, slot):
        r0 = pl.multiple_of(j * tk, tk)
        kj = k_ref[0, pl.ds(r0, tk), (g // 2) * LANES:(g // 2 + 1) * LANES]
        st_ref[slot] = jnp.dot(kj, qt_ref[0, g], preferred_element_type=F32)

    def softmax_pv(j, g, slot, masked):
        st = st_ref[slot]
        if masked:
            keep = lax.broadcasted_iota(jnp.int32, (tk, tq), 0) <= lax.broadcasted_iota(jnp.int32, (tk, tq), 1)
            st = jnp.where(keep, st, NEG_INF)
        m_prev = m_ref[g]
        m_new = jnp.maximum(m_prev, jnp.max(st, axis=0, keepdims=True))
        pt_ref[g] = jnp.exp2(st - jnp.concatenate([m_new] * (tk // 8), axis=0)).astype(MXU_DTYPE)
        m_ref[g] = m_new
        alpha = jnp.exp2(m_prev - m_new)
        upd = jnp.dot(vt_ref[0, g // 2, j], pt_ref[g], preferred_element_type=F32)
        acc_ref[g] = jnp.concatenate([alpha] * (VT_ROWS // 8), axis=0) * acc_ref[g] + upd

    def block(j, masked):
        for g in range(ng):
            if g < ng - 1:
                scores(j, g + 1, (g + 1) % 2)
            elif not masked:
                scores(j + 1, 0, 0)
            softmax_pv(j, g, g % 2, masked)

    scores(0, 0, 0)

    def body(j, carry):
        block(j, False)
        return carry

    lax.fori_loop(0, i, body, 0)
    block(i, True)

    lp = lp_ref[...]
    lam = (jnp.exp(jnp.sum(lp[0:1] * lp[1:2], axis=-1, keepdims=True))
           - jnp.exp(jnp.sum(lp[2:3] * lp[3:4], axis=-1, keepdims=True)) + lam_init)
    for hh in range(ng // 2):
        a1, a2 = acc_ref[2 * hh], acc_ref[2 * hh + 1]
        o1 = a1[0:DIFF_DV] / a1[DIFF_DV:DIFF_DV + 1]
        o2 = a2[0:DIFF_DV] / a2[DIFF_DV:DIFF_DV + 1]
        y = _rms_lanes((o1 - lam * o2).T) * nw_ref[...] * (1.0 - lam_init)
        sl = slice(hh * LANES, (hh + 1) * LANES)
        o_ref[0, :, sl] = (y * z_ref[0, :, sl]).astype(o_ref.dtype)


def _ret_kernel(u_ref, w_ref, cos_ref, sin_ref, dec_ref, qw_ref, kw_ref, cd_ref, msk_ref,
                o_ref, st_ref, proj_ref):
    ts = u_ref.shape[1]

    @pl.when(pl.program_id(1) == 0)
    def _():
        st_ref[...] = jnp.zeros(st_ref.shape, F32)

    hid = (lax.broadcasted_iota(jnp.int32, (1, 256), 1) % LANES) // 32

    proj_ref[...] = jnp.dot(u_ref[0], w_ref[...], preferred_element_type=F32)

    def chunk(c, carry):
        rows = slice(c * CHUNK, (c + 1) * CHUNK)
        cos, sin = cos_ref[rows, :], sin_ref[rows, :]
        qr = _rope(proj_ref[rows, 0:256], cos, sin)
        kr = _rope(proj_ref[rows, 256:512], cos, sin) * (RET_DK ** -0.5)
        v = proj_ref[rows, 512:1024].astype(MXU_DTYPE)
        krb = kr.astype(MXU_DTYPE)
        state = st_ref[...]
        inter = _dot(qr * qw_ref[...], state)
        st_ref[...] = state * cd_ref[...] + _dot_tn(kr * kw_ref[...], v) * msk_ref[...]
        for h in range(RET_HEADS):
            sl = slice(h * RET_DV, (h + 1) * RET_DV)
            sc = _dot_nt(jnp.where(hid == h, qr, 0.0), krb) * dec_ref[h]
            y = _rms_lanes(_dot(sc, v[:, sl]) + inter[:, sl])
            z = proj_ref[rows, 1024 + h * RET_DV:1024 + (h + 1) * RET_DV]
            o_ref[0, rows, sl] = (y * _silu(z)).astype(o_ref.dtype)
        return carry

    for c in range(ts // CHUNK):
        chunk(c, 0)


def _ssd_kernel(u_ref, w_ref, cw_ref, cb_ref, dtb_ref, alog_ref, dskip_ref, onw_ref,
                o_ref, st_ref, ext_ref, act_ref, proj_ref):
    ts = u_ref.shape[1]
    first = pl.program_id(1) == 0

    @pl.when(first)
    def _():
        st_ref[...] = jnp.zeros(st_ref.shape, F32)
        ext_ref[0:CONV_PAD, :] = jnp.zeros((CONV_PAD, ext_ref.shape[1]), F32)

    a_neg = -jnp.exp(alog_ref[...])
    tri = _tri(CHUNK)
    tri_b = tri.astype(jnp.bfloat16)
    lane = lax.broadcasted_iota(jnp.int32, (1, LANES), 1)
    gw = GROUP_W // SSD_GROUPS

    def project(c):
        rows = slice(c * CHUNK, (c + 1) * CHUNK)
        proj_ref[rows, :] = jnp.dot(u_ref[0, rows, :], w_ref[...], preferred_element_type=F32)

    project(0)

    def chunk(c, carry):
        if c + 1 < ts // CHUNK:
            project(c + 1)
        r0 = c * CHUNK
        rows = slice(r0, r0 + CHUNK)
        act_ref[rows, :] = _causal_conv_silu(ext_ref, proj_ref[rows, 0:SSD_XBC], r0, cw_ref, cb_ref)
        dt = _softplus(proj_ref[rows, SSD_XBC:SSD_XBC + GROUP_W] + dtb_ref[...])
        z = _silu(proj_ref[rows, SSD_XBC + GROUP_W:SSD_XBC + 2 * GROUP_W])
        xs = act_ref[rows, 0:GROUP_W]
        cs = _cumsum_rows(dt * a_neg, tri_b)
        cs_last = cs[CHUNK - 1:CHUNK, :]
        xdt = xs * dt
        y_parts = []
        for g in range(SSD_GROUPS):
            bm = act_ref[rows, GROUP_W + g * SSD_STATE:GROUP_W + (g + 1) * SSD_STATE]
            cm = act_ref[rows, GROUP_W + (SSD_GROUPS + g) * SSD_STATE:GROUP_W + (SSD_GROUPS + g + 1) * SSD_STATE]
            cb = _dot_nt(cm, bm)
            gs = slice(g * gw, (g + 1) * gw)
            prev = st_ref[g]
            y_off = _dot(cm, prev) * jnp.exp(cs[:, gs])
            st_ref[g] = prev * jnp.exp(cs_last[:, gs]) + _dot_tn(bm, xdt[:, gs] * jnp.exp(cs_last[:, gs] - cs[:, gs]))
            for pr in range(gw // LANES):
                ls = slice(g * gw + pr * LANES, g * gw + (pr + 1) * LANES)
                cs_t = cs[:, ls].T
                xdt_pair = xdt[:, ls]
                y_pair = None
                for e in range(2):
                    col = cs[:, ls][:, e * SSD_HEAD_DIM:e * SSD_HEAD_DIM + 1]
                    row = cs_t[e * SSD_HEAD_DIM:e * SSD_HEAD_DIM + 1, :]
                    lmat = jnp.exp(jnp.where(tri, col - row, -jnp.inf))
                    half = jnp.where((lane // SSD_HEAD_DIM) == e, xdt_pair, 0.0)
                    term = _dot(cb * lmat, half)
                    y_pair = term if y_pair is None else y_pair + term
                y_parts.append(y_pair + y_off[:, pr * LANES:(pr + 1) * LANES])
        y = jnp.concatenate(y_parts, axis=-1) + xs * dskip_ref[...]
        o_ref[0, rows, :] = (_rms_lanes(y * z) * onw_ref[...]).astype(o_ref.dtype)
        return carry

    for c in range(ts // CHUNK):
        chunk(c, 0)
    ext_ref[0:CONV_PAD, :] = ext_ref[ts:ts + CONV_PAD, :]


def _mlstm_kernel(u_ref, w_ref, cw_ref, cb_ref, ib_ref, fb_ref, onw_ref,
                  o_ref, ct_ref, n_ref, m_ref, ext_ref, act_ref, proj_ref):
    ts = u_ref.shape[1]
    first = pl.program_id(1) == 0
    nh, dh = MLSTM_HEADS, MLSTM_DH

    @pl.when(first)
    def _():
        ct_ref[...] = jnp.zeros(ct_ref.shape, F32)
        n_ref[...] = jnp.zeros(n_ref.shape, F32)
        m_ref[...] = jnp.zeros(m_ref.shape, F32)

    @pl.when(first)
    def _():
        ext_ref[0:CONV_PAD, :] = jnp.zeros((CONV_PAD, ext_ref.shape[1]), F32)

    tri = _tri(CHUNK)
    tri_b = tri.astype(jnp.bfloat16)
    c_v, c_o, c_i, c_f, c_z = (2 * GROUP_W + k * GROUP_W for k in range(5))

    def project(c):
        rows = slice(c * CHUNK, (c + 1) * CHUNK)
        proj_ref[rows, :] = jnp.dot(u_ref[0, rows, :], w_ref[...], preferred_element_type=F32)

    project(0)

    def chunk(c, carry):
        if c + 1 < ts // CHUNK:
            project(c + 1)
        r0 = c * CHUNK
        rows = slice(r0, r0 + CHUNK)
        act_ref[rows, :] = _causal_conv_silu(ext_ref, proj_ref[rows, 0:2 * GROUP_W], r0, cw_ref, cb_ref)
        ig = proj_ref[rows, c_i:c_i + GROUP_W] + ib_ref[...]
        fg = -_softplus(-(proj_ref[rows, c_f:c_f + GROUP_W] + fb_ref[...]))
        bcs = _cumsum_rows(fg, tri_b)
        for h in range(nh):
            sl = slice(h * dh, (h + 1) * dh)
            b_col = bcs[:, sl]
            b_row = b_col.T
            i_col = ig[:, sl]
            d_log = jnp.where(tri, b_col - b_row + i_col.T, -jnp.inf)
            m_prev = m_ref[h:h + 1, :]
            inter_log = b_col + m_prev
            m_row = jnp.maximum(jnp.max(d_log, axis=-1, keepdims=True), inter_log)
            qh = act_ref[rows, sl]
            kh = act_ref[rows, GROUP_W + h * dh:GROUP_W + (h + 1) * dh] * (dh ** -0.5)
            vh = proj_ref[rows, c_v + h * dh:c_v + (h + 1) * dh].astype(MXU_DTYPE)
            s = _dot_nt(qh, kh) * jnp.exp(d_log - m_row)
            inter_w = jnp.exp(inter_log - m_row)
            ct = ct_ref[h]
            n_row = n_ref[h:h + 1, :]
            num = _dot(s, vh) + inter_w * _dot(qh, ct)
            qn = jnp.sum(s, axis=-1, keepdims=True) + inter_w * jnp.sum(qh * n_row, axis=-1, keepdims=True)
            hv = num / jnp.maximum(jnp.abs(qn), jnp.exp(-m_row))
            b_last = b_col[CHUNK - 1:CHUNK, :]
            w_log = b_last - b_col + i_col
            m_new = jnp.maximum(b_last + m_prev, jnp.max(w_log, axis=0, keepdims=True))
            kw = kh * jnp.exp(w_log - m_new)
            decay = jnp.exp(b_last + m_prev - m_new)
            ct_ref[h] = decay * ct + _dot_tn(kw, vh)
            n_ref[h:h + 1, :] = decay * n_row + jnp.sum(kw, axis=0, keepdims=True)
            m_ref[h:h + 1, :] = m_new
            hv = _sigmoid(proj_ref[rows, c_o + h * dh:c_o + (h + 1) * dh]) * hv
            hv = _rms_lanes(hv - jnp.mean(hv, axis=-1, keepdims=True)) * onw_ref[:, sl]
            o_ref[0, rows, sl] = (hv * _silu(proj_ref[rows, c_z + h * dh:c_z + (h + 1) * dh])).astype(o_ref.dtype)
        return carry

    for c in range(ts // CHUNK):
        chunk(c, 0)
    ext_ref[0:CONV_PAD, :] = ext_ref[ts:ts + CONV_PAD, :]


def _out_kernel(last, h_ref, a_ref, b_ref, c_ref, d_ref, w_ref, nw_ref, *o_refs):
    acc = h_ref[0]
    for k, m_ref in enumerate((a_ref, b_ref, c_ref, d_ref)):
        acc = acc + jnp.dot(m_ref[0], w_ref[k * GROUP_W:(k + 1) * GROUP_W, :], preferred_element_type=F32)
    normed = _rms_lanes(acc) * nw_ref[...]
    if last:
        o_refs[0][0] = normed
    else:
        o_refs[0][0] = acc
        o_refs[1][0] = normed.astype(o_refs[1].dtype)


def _params(*sem):
    return pltpu.CompilerParams(dimension_semantics=sem, vmem_limit_bytes=VMEM_LIMIT)


def _const_spec(shape):
    nd = len(shape)
    return pl.BlockSpec(shape, lambda *_: (0,) * nd)


def _retention_tables():
    nh, L = RET_HEADS, CHUNK
    log_g = jnp.log(1.0 - jnp.exp2(-5.0 - jnp.arange(nh, dtype=F32)))
    pos = jnp.arange(L, dtype=F32)
    rel = pos[:, None] - pos[None, :]
    decay = jnp.where(rel >= 0, jnp.exp(log_g[:, None, None] * jnp.maximum(rel, 0.0)), 0.0)
    head_of_col = (np.arange(256) % LANES) // 32
    q_w = jnp.exp(log_g[None, :] * (pos + 1.0)[:, None])[:, head_of_col]
    k_w = jnp.exp(log_g[None, :] * (L - 1.0 - pos)[:, None])[:, head_of_col]
    same_head = jnp.asarray(head_of_col[:, None] == (np.arange(GROUP_W) // RET_DV)[None, :], F32)
    chunk_decay = jnp.exp(log_g * L)[head_of_col][:, None] * same_head
    return decay, q_w, k_w, chunk_decay, same_head


def kernel(x, norm_w, w_in, w_out, diff_lambda, diff_norm_w, ssd_conv_w, ssd_conv_b, ssd_dt_bias, ssd_a_log,
           ssd_d, ssd_norm_w, mlstm_conv_w, mlstm_conv_b, mlstm_gate_b, mlstm_norm_w, final_norm_w):
    bsz, seq, dm = x.shape
    depth = w_in.shape[0]
    ts = min(TOKEN_BLOCK, seq)
    tq = ts
    nblk = seq // ts
    bf = MXU_DTYPE

    inv = ROPE_THETA ** (-jnp.arange(0, RET_DK, 2, dtype=F32) / RET_DK)
    ang = jnp.arange(seq, dtype=F32)[:, None] * inv[None, :]
    cos4, sin4 = jnp.tile(jnp.cos(ang), (1, 4)), jnp.tile(jnp.sin(ang), (1, 4))
    ret_tabs = _retention_tables()

    x_spec = pl.BlockSpec((1, ts, dm), lambda b, s: (b, s, 0))
    row_spec = pl.BlockSpec((ts, LANES), lambda b, s: (s, 0))
    grp_spec = pl.BlockSpec((1, ts, GROUP_W), lambda b, s: (b, s, 0))
    grp_shape = jax.ShapeDtypeStruct((bsz, seq, GROUP_W), bf)
    u_shape = jax.ShapeDtypeStruct((bsz, seq, dm), bf)

    def layer_spec(shape, l):
        return pl.BlockSpec((None,) + tuple(shape), lambda *_: (l,) + (0,) * len(shape),
                            pipeline_mode=pl.Buffered(1))

    cols = lambda lo, n: w_in[:, :, lo:lo + n]
    expand = lambda a, n: jnp.repeat(a, n, axis=-1)
    cat = lambda parts: jnp.concatenate(parts, axis=-1).astype(bf)

    def ret_qk(lo):
        return cols(lo, 256).reshape(depth, dm, RET_HEADS, 2, 32).swapaxes(2, 3).reshape(depth, dm, 256)

    def diff_qk(lo):
        return cols(lo, 512).reshape(depth, dm, DIFF_HEADS, 2, 2, 32).swapaxes(3, 4).reshape(depth, dm, 512)

    g0 = MLSTM_OFF + 4 * GROUP_W
    w_diff = cat([diff_qk(DIFF_OFF), diff_qk(DIFF_OFF + 512), cols(DIFF_OFF + 1024, 1024)])
    w_ret = cat([ret_qk(RET_OFF), ret_qk(RET_OFF + 256), cols(RET_OFF + 512, 1024)])
    w_ssd = cat([cols(SSD_OFF, SSD_XBC), expand(cols(SSD_OFF + SSD_XBC, SSD_HEADS), SSD_HEAD_DIM),
                 cols(SSD_OFF + SSD_XBC + SSD_HEADS, GROUP_W)])
    w_ml = cat([cols(MLSTM_OFF, 4 * GROUP_W), expand(cols(g0, 2 * MLSTM_HEADS), MLSTM_DH),
                cols(g0 + 2 * MLSTM_HEADS, GROUP_W)])
    w_out_b = w_out.astype(bf)
    rep = lambda a, n: jnp.repeat(a.astype(F32), n, axis=-1)[:, None, :]
    row = lambda a: a.astype(F32)[:, None, :]
    norm_rows = row(jnp.concatenate([norm_w, final_norm_w[None]], axis=0))
    ssd_rows = [rep(a, SSD_HEAD_DIM) for a in (ssd_dt_bias, ssd_a_log, ssd_d)] + [row(ssd_norm_w)]
    ml_rows = [rep(mlstm_gate_b[:, :MLSTM_HEADS], MLSTM_DH), rep(mlstm_gate_b[:, MLSTM_HEADS:], MLSTM_DH),
               row(mlstm_norm_w)]

    u = pl.pallas_call(
        _norm_kernel, grid=(bsz, nblk), in_specs=[x_spec, layer_spec((1, dm), 0)], out_specs=x_spec,
        out_shape=u_shape, compiler_params=_params("parallel", "parallel"), name="first_norm",
    )(x, norm_rows)

    h = x
    for l in range(depth):
        nkb = nblk
        dqt, dk, dvt, dz = pl.pallas_call(
            _diff_proj_kernel,
            grid=(bsz, nblk),
            in_specs=[x_spec, layer_spec(w_diff.shape[1:], l), row_spec, row_spec],
            out_specs=[pl.BlockSpec((1, 2 * DIFF_HEADS, LANES, ts), lambda b, s: (b, 0, 0, s)),
                       grp_spec,
                       pl.BlockSpec((1, DIFF_HEADS, 1, VT_ROWS, ts), lambda b, s: (b, 0, s, 0, 0)),
                       grp_spec],
            out_shape=[jax.ShapeDtypeStruct((bsz, 2 * DIFF_HEADS, LANES, seq), bf), grp_shape,
                       jax.ShapeDtypeStruct((bsz, DIFF_HEADS, nkb, VT_ROWS, ts), bf),
                       jax.ShapeDtypeStruct((bsz, seq, GROUP_W), F32)],
            compiler_params=_params("parallel", "parallel"),
            name="diff_proj",
        )(u, w_diff, cos4, sin4)

        lam_init = 0.8 - 0.6 * math.exp(-0.3 * l)
        hps = FLASH_HEADS_PER_STEP
        heads_q = pl.BlockSpec((1, tq, hps * DIFF_DV), lambda b, p, i: (b, i, p))
        diff_out = pl.pallas_call(
            functools.partial(_diff_flash_kernel, lam_init),
            grid=(bsz, DIFF_HEADS // hps, seq // tq),
            in_specs=[pl.BlockSpec((1, 2 * hps, LANES, tq), lambda b, p, i: (b, p, 0, i)),
                      pl.BlockSpec((1, seq, hps * LANES), lambda b, p, i: (b, 0, p)),
                      pl.BlockSpec((1, hps, nkb, VT_ROWS, ts), lambda b, p, i: (b, p, 0, 0, 0)),
                      heads_q, layer_spec((4, DIFF_DK), l), layer_spec((1, DIFF_DV), l)],
            out_specs=heads_q,
            out_shape=grp_shape,
            scratch_shapes=[pltpu.VMEM((2 * hps, 8, tq), F32),
                            pltpu.VMEM((2 * hps, VT_ROWS, tq), F32), pltpu.VMEM((2, ts, tq), F32),
                            pltpu.VMEM((2 * hps, ts, tq), bf)],
            compiler_params=_params("parallel", "parallel", "arbitrary"),
            name="diff_flash",
        )(dqt, dk, dvt, dz, diff_lambda.astype(F32), row(diff_norm_w))

        ret_out = pl.pallas_call(
            _ret_kernel,
            grid=(bsz, nblk),
            in_specs=[x_spec, layer_spec(w_ret.shape[1:], l), row_spec, row_spec]
                     + [_const_spec(t.shape) for t in ret_tabs],
            out_specs=grp_spec,
            out_shape=grp_shape,
            scratch_shapes=[pltpu.VMEM((256, GROUP_W), F32), pltpu.VMEM((ts, w_ret.shape[2]), F32)],
            compiler_params=_params("parallel", "arbitrary"),
            name="retention",
        )(u, w_ret, cos4, sin4, *ret_tabs)

        ssd_out = pl.pallas_call(
            _ssd_kernel,
            grid=(bsz, nblk),
            in_specs=[x_spec, layer_spec(w_ssd.shape[1:], l), layer_spec((SSD_CONV, SSD_XBC), l),
                      layer_spec((1, SSD_XBC), l)] + [layer_spec((1, GROUP_W), l)] * 4,
            out_specs=grp_spec,
            out_shape=grp_shape,
            scratch_shapes=[pltpu.VMEM((SSD_GROUPS, SSD_STATE, GROUP_W // SSD_GROUPS), F32),
                            pltpu.VMEM((ts + CONV_PAD, SSD_XBC), F32), pltpu.VMEM((ts, SSD_XBC), F32),
                            pltpu.VMEM((ts, w_ssd.shape[2]), F32)],
            compiler_params=_params("parallel", "arbitrary"),
            name="ssd",
        )(u, w_ssd, ssd_conv_w.astype(F32), row(ssd_conv_b), *ssd_rows)

        ml_out = pl.pallas_call(
            _mlstm_kernel,
            grid=(bsz, nblk),
            in_specs=[x_spec, layer_spec(w_ml.shape[1:], l), layer_spec((MLSTM_CONV, 2 * GROUP_W), l),
                      layer_spec((1, 2 * GROUP_W), l)] + [layer_spec((1, GROUP_W), l)] * 3,
            out_specs=grp_spec,
            out_shape=grp_shape,
            scratch_shapes=[pltpu.VMEM((MLSTM_HEADS, MLSTM_DH, MLSTM_DH), F32), pltpu.VMEM((8, MLSTM_DH), F32),
                            pltpu.VMEM((8, LANES), F32), pltpu.VMEM((ts + CONV_PAD, 2 * GROUP_W), F32),
                            pltpu.VMEM((ts, 2 * GROUP_W), F32), pltpu.VMEM((ts, w_ml.shape[2]), F32)],
            compiler_params=_params("parallel", "arbitrary"),
            name="mlstm",
        )(u, w_ml, mlstm_conv_w.astype(F32), row(mlstm_conv_b), *ml_rows)

        last = l == depth - 1
        h_shape = jax.ShapeDtypeStruct((bsz, seq, dm), F32)
        outs = pl.pallas_call(
            functools.partial(_out_kernel, last),
            grid=(bsz, nblk),
            in_specs=[x_spec] + [grp_spec] * 4 + [layer_spec((4 * GROUP_W, dm), l), layer_spec((1, dm), l + 1)],
            out_specs=[x_spec] if last else [x_spec, x_spec],
            out_shape=[h_shape] if last else [h_shape, u_shape],
            compiler_params=_params("parallel", "parallel"),
            name="out_proj",
        )(h, ret_out, diff_out, ssd_out, ml_out, w_out_b, norm_rows)
        h = outs[0]
        if not last:
            u = outs[1]
    return h
```
